```python
import math
import jax, jax.numpy as jnp
from jax import lax
import numpy as np

D_MODEL = 2048
BATCH = 1
SEQ = 16384
DEPTH = 1

CHUNK = 64
Q_BLOCK = 128
D_CONV = D_MODEL // 2
D_ATTN = D_MODEL - D_CONV
N_HEADS = 8
HEAD_DIM = D_ATTN // N_HEADS // 2
V_DIM = 2 * HEAD_DIM
CONV_WIDTH = 31
N_GROUPS = 4
EXPERTS_PER_GROUP = 8
N_EXPERTS = N_GROUPS * EXPERTS_PER_GROUP
TOP_K = 2
D_EXPERT = 512
D_Q = N_HEADS * 2 * HEAD_DIM
D_V = N_HEADS * V_DIM
D_IN = 2 * D_CONV + 2 * D_Q + D_V
EPS = 1e-6

kernel_name = "hymba_conformer_diffattn_hiermoe_block"


def rms_norm(x, g):
    xf = x.astype(jnp.float32)
    y = xf * lax.rsqrt(jnp.mean(xf * xf, axis=-1, keepdims=True) + EPS)
    return (y * g.astype(jnp.float32)).astype(x.dtype)


def layer_norm(x, g, b):
    xf = x.astype(jnp.float32)
    mu = jnp.mean(xf, axis=-1, keepdims=True)
    xc = xf - mu
    y = xc * lax.rsqrt(jnp.mean(xc * xc, axis=-1, keepdims=True) + EPS)
    return (y * g.astype(jnp.float32) + b.astype(jnp.float32)).astype(x.dtype)


def lambda_init(layer_idx):
    return 0.8 - 0.6 * math.exp(-0.3 * layer_idx)


def conformer_conv(u, dw_kernel, dw_bias, ln_g, ln_b):
    a, b = jnp.split(u, 2, axis=-1)
    h = a * jax.nn.sigmoid(b)
    h = lax.conv_general_dilated(
        h, dw_kernel[:, None, :], window_strides=(1,),
        padding=[(CONV_WIDTH - 1, 0)],
        dimension_numbers=("NWC", "WIO", "NWC"),
        feature_group_count=D_CONV) + dw_bias
    h = layer_norm(h, ln_g, ln_b)
    return jax.nn.silu(h)


def diff_attention(qkv, q_norm_g, k_norm_g, lq1, lk1, lq2, lk2, subln_g, lam_init):
    B, S, _ = qkv.shape
    q, k, v = jnp.split(qkv, [D_Q, 2 * D_Q], axis=-1)
    q = q.reshape(B, S, N_HEADS, 2, HEAD_DIM)
    k = k.reshape(B, S, N_HEADS, 2, HEAD_DIM)
    v = v.reshape(B, S, N_HEADS, V_DIM)
    q = rms_norm(q, q_norm_g) * (HEAD_DIM ** -0.5)
    k = rms_norm(k, k_norm_g)
    lam = (jnp.exp(jnp.sum(lq1.astype(jnp.float32) * lk1.astype(jnp.float32)))
           - jnp.exp(jnp.sum(lq2.astype(jnp.float32) * lk2.astype(jnp.float32)))
           + lam_init)
    n_blk = S // Q_BLOCK
    qb = jnp.moveaxis(q.reshape(B, n_blk, Q_BLOCK, N_HEADS, 2, HEAD_DIM), 1, 0)
    k_chunk = jnp.arange(S) // CHUNK

    def one_block(args):
        i, q_i = args
        q_chunk = (i * Q_BLOCK + jnp.arange(Q_BLOCK)) // CHUNK
        mask = k_chunk[None, :] <= q_chunk[:, None]
        s = jnp.einsum("bqhrd,bkhrd->bhrqk", q_i, k).astype(jnp.float32)
        s = jnp.where(mask, s, -jnp.inf)
        p = jax.nn.softmax(s, axis=-1)
        a = p[:, :, 0] - lam * p[:, :, 1]
        return jnp.einsum("bhqk,bkhd->bqhd", a.astype(v.dtype), v)

    o = lax.map(one_block, (jnp.arange(n_blk), qb))
    o = jnp.moveaxis(o, 0, 1).reshape(B, S, N_HEADS, V_DIM)
    o = rms_norm(o, subln_g) * (1.0 - lam_init)
    return o.reshape(B, S, D_V)


def hier_moe(x, w_group, b_group, w_router, b_router, w_gate, w_up, w_down):
    B, S, D = x.shape
    t = x.reshape(B * S, D)
    g_logits = (t @ w_group + b_group).astype(jnp.float32)
    g_prob = jax.nn.softmax(g_logits, axis=-1)
    g_idx = jnp.argmax(g_logits, axis=-1)
    g_w = jnp.take_along_axis(g_prob, g_idx[:, None], axis=-1)
    e_logits = (jnp.einsum("nd,gde->nge", t, w_router) + b_router).astype(jnp.float32)
    e_logits = jnp.take_along_axis(e_logits, g_idx[:, None, None], axis=1)[:, 0]
    top_v, top_i = lax.top_k(e_logits, TOP_K)
    top_p = jax.nn.softmax(top_v, axis=-1)
    expert_id = g_idx[:, None] * EXPERTS_PER_GROUP + top_i
    combine = jnp.sum(jax.nn.one_hot(expert_id, N_EXPERTS, dtype=jnp.float32)
                      * (g_w * top_p)[..., None], axis=1)
    out = jnp.zeros_like(t)
    for e in range(N_EXPERTS):
        h = jax.nn.silu(t @ w_gate[e]) * (t @ w_up[e])
        out = out + combine[:, e:e + 1].astype(t.dtype) * (h @ w_down[e])
    return out.reshape(B, S, D)


def setup_inputs(seed: int = 0) -> dict:
    key = jax.random.key(seed)
    ks = jax.random.split(key, 24)
    nrm = lambda k, shape, s: jax.random.normal(k, shape, jnp.float32) * s
    L = DEPTH
    return {
        "x": nrm(ks[0], (BATCH, SEQ, D_MODEL), 1.0),
        "norm1_g": 1.0 + nrm(ks[1], (L, D_MODEL), 0.02),
        "w_in": nrm(ks[2], (L, D_MODEL, D_IN), D_MODEL ** -0.5),
        "conv_dw_kernel": nrm(ks[3], (L, CONV_WIDTH, D_CONV), CONV_WIDTH ** -0.5),
        "conv_dw_bias": nrm(ks[4], (L, D_CONV), 0.02),
        "conv_ln_g": 1.0 + nrm(ks[5], (L, D_CONV), 0.02),
        "conv_ln_b": nrm(ks[6], (L, D_CONV), 0.02),
        "q_norm_g": 1.0 + nrm(ks[7], (L, 2, HEAD_DIM), 0.02),
        "k_norm_g": 1.0 + nrm(ks[8], (L, 2, HEAD_DIM), 0.02),
        "lambda_q1": nrm(ks[9], (L, HEAD_DIM), 0.1),
        "lambda_k1": nrm(ks[10], (L, HEAD_DIM), 0.1),
        "lambda_q2": nrm(ks[11], (L, HEAD_DIM), 0.1),
        "lambda_k2": nrm(ks[12], (L, HEAD_DIM), 0.1),
        "subln_g": 1.0 + nrm(ks[13], (L, V_DIM), 0.02),
        "w_out": nrm(ks[14], (L, D_CONV + D_V, D_MODEL), (D_CONV + D_V) ** -0.5),
        "norm2_g": 1.0 + nrm(ks[15], (L, D_MODEL), 0.02),
        "w_group": nrm(ks[16], (L, D_MODEL, N_GROUPS), D_MODEL ** -0.5),
        "b_group": nrm(ks[17], (L, N_GROUPS), 0.01),
        "w_router": nrm(ks[18], (L, N_GROUPS, D_MODEL, EXPERTS_PER_GROUP), D_MODEL ** -0.5),
        "b_router": nrm(ks[19], (L, N_GROUPS, EXPERTS_PER_GROUP), 0.01),
        "w_gate": nrm(ks[20], (L, N_EXPERTS, D_MODEL, D_EXPERT), D_MODEL ** -0.5),
        "w_up": nrm(ks[21], (L, N_EXPERTS, D_MODEL, D_EXPERT), D_MODEL ** -0.5),
        "w_down": nrm(ks[22], (L, N_EXPERTS, D_EXPERT, D_MODEL), D_EXPERT ** -0.5),
    }


def reference(x, norm1_g, w_in, conv_dw_kernel, conv_dw_bias, conv_ln_g, conv_ln_b,
              q_norm_g, k_norm_g, lambda_q1, lambda_k1, lambda_q2, lambda_k2, subln_g,
              w_out, norm2_g, w_group, b_group, w_router, b_router, w_gate, w_up, w_down):
    h = x
    for l in range(DEPTH):
        u = rms_norm(h, norm1_g[l]) @ w_in[l]
        u_conv, u_attn = jnp.split(u, [2 * D_CONV], axis=-1)
        y_conv = conformer_conv(u_conv, conv_dw_kernel[l], conv_dw_bias[l],
                                conv_ln_g[l], conv_ln_b[l])
        y_attn = diff_attention(u_attn, q_norm_g[l], k_norm_g[l], lambda_q1[l], lambda_k1[l],
                                lambda_q2[l], lambda_k2[l], subln_g[l], lambda_init(l))
        h = h + jnp.concatenate([y_conv, y_attn], axis=-1) @ w_out[l]
        h = h + hier_moe(rms_norm(h, norm2_g[l]), w_group[l], b_group[l], w_router[l],
                         b_router[l], w_gate[l], w_up[l], w_down[l])
    return h
```

```python
import functools
import math

import jax
import jax.numpy as jnp
from jax import lax
from jax.experimental import pallas as pl
from jax.experimental.pallas import tpu as pltpu

D_MODEL = 2048
D_CONV = 1024
N_HEADS = 8
HEAD_DIM = 64
V_DIM = 128
CONV_WIDTH = 31
CHUNK = 64
N_GROUPS = 4
EXPERTS_PER_GROUP = 8
N_EXPERTS = 32
TOP_K = 2
D_EXPERT = 512
D_Q = 1024
D_V = 1024
D_IN = 2 * D_CONV + 2 * D_Q + D_V
EPS = 1e-6
LAMBDA_INIT = 0.8 - 0.6 * math.exp(-0.3 * 0)

LANES = 128
VMEM_LIMIT = 56 * 1024 * 1024
NEG_BIG = -1e30

TM_PROJ = 512
TK_ATTN = 512
TQ_ATTN = 512
TS_CONV = 128
RC_CONV = 16
HALO = 32
TB_RANK = 512
TB_ROWS = 256
TM_EXP = 256

bf16 = jnp.bfloat16
f32 = jnp.float32


def _cparams(n_axes):
    return pltpu.CompilerParams(
        dimension_semantics=("arbitrary",) * n_axes, vmem_limit_bytes=VMEM_LIMIT)


def _inproj_body(x_ref, g1_ref, w_ref, qg_ref, kg_ref, bd_ref,
                 hglu_ref, q_ref, kt_ref, v_ref):
    x = x_ref[...]
    ms = jnp.mean(x * x, axis=-1, keepdims=True)
    xn = (x * lax.rsqrt(ms + EPS) * g1_ref[...]).astype(bf16)

    def mm(c0, n):
        return jnp.dot(xn, w_ref[:, c0:c0 + n], preferred_element_type=f32)

    nt = 512
    for c in range(0, D_CONV, nt):
        a = mm(c, nt)
        b = mm(D_CONV + c, nt)
        hglu_ref[:, c:c + nt] = a * jax.nn.sigmoid(b)

    bd = bd_ref[...]
    sl = 256

    def qk_norm(c0, gain):
        u = mm(c0, sl)
        ms64 = jnp.dot((u * u).astype(bf16), bd, preferred_element_type=f32)
        return u * lax.rsqrt(ms64 + EPS) * gain

    for c in range(0, D_Q, sl):
        q_ref[:, c:c + sl] = qk_norm(2 * D_CONV + c, qg_ref[:, c:c + sl]).astype(bf16)
    for c in range(0, D_Q, sl):
        kn = qk_norm(2 * D_CONV + D_Q + c, kg_ref[:, c:c + sl])
        kt_ref[0, c:c + sl, :] = kn.T.astype(bf16)
    for c in range(0, D_V, nt):
        v_ref[:, c:c + nt] = mm(2 * D_CONV + 2 * D_Q + c, nt).astype(bf16)


def _inproj(x2, g1, w_bf, qg, kg, bd):
    s = x2.shape[0]
    tm = TM_PROJ
    assert tm == TK_ATTN and s % tm == 0
    const = lambda i: (0, 0)
    return pl.pallas_call(
        _inproj_body,
        grid=(s // tm,),
        in_specs=[
            pl.BlockSpec((tm, D_MODEL), lambda i: (i, 0)),
            pl.BlockSpec((1, D_MODEL), const),
            pl.BlockSpec((D_MODEL, D_IN), const, pipeline_mode=pl.Buffered(1)),
            pl.BlockSpec((1, D_Q), const),
            pl.BlockSpec((1, D_Q), const),
            pl.BlockSpec((256, 256), const),
        ],
        out_specs=[
            pl.BlockSpec((tm, D_CONV), lambda i: (i, 0)),
            pl.BlockSpec((tm, D_Q), lambda i: (i, 0)),
            pl.BlockSpec((1, D_Q, tm), lambda i: (i, 0, 0)),
            pl.BlockSpec((tm, D_V), lambda i: (i, 0)),
        ],
        out_shape=[
            jax.ShapeDtypeStruct((s, D_CONV), f32),
            jax.ShapeDtypeStruct((s, D_Q), bf16),
            jax.ShapeDtypeStruct((s // tm, D_Q, tm), bf16),
            jax.ShapeDtypeStruct((s, D_V), bf16),
        ],
        compiler_params=_cparams(1),
        name="inproj",
    )(x2, g1, w_bf, qg, kg, bd)


def _conv_body(prev_ref, cur_ref, w_ref, b_ref, lg_ref, lb_ref, o_ref, pad_ref):
    i = pl.program_id(0)
    pad_ref[0:HALO, :] = jnp.where(i > 0, prev_ref[...], 0.0)
    pad_ref[HALO:HALO + TS_CONV, :] = cur_ref[...]
    off = HALO - (CONV_WIDTH - 1)
    for r0 in range(0, TS_CONV, RC_CONV):
        acc = jnp.broadcast_to(b_ref[...], (RC_CONV, D_CONV))
        for j in range(CONV_WIDTH):
            acc = acc + w_ref[j:j + 1, :] * pad_ref[r0 + off + j:r0 + off + j + RC_CONV, :]
        mu = jnp.mean(acc, axis=-1, keepdims=True)
        xc = acc - mu
        var = jnp.mean(xc * xc, axis=-1, keepdims=True)
        y = xc * lax.rsqrt(var + EPS) * lg_ref[...] + lb_ref[...]
        o_ref[r0:r0 + RC_CONV, :] = (y * jax.nn.sigmoid(y)).astype(bf16)


def _conv(hglu, w, b, lg, lb):
    s = hglu.shape[0]
    ts = TS_CONV
    const = lambda i: (0, 0)
    return pl.pallas_call(
        _conv_body,
        grid=(s // ts,),
        in_specs=[
            pl.BlockSpec((HALO, D_CONV), lambda i: (jnp.maximum(i * (ts // HALO) - 1, 0), 0)),
            pl.BlockSpec((ts, D_CONV), lambda i: (i, 0)),
            pl.BlockSpec((CONV_WIDTH, D_CONV), const),
            pl.BlockSpec((1, D_CONV), const),
            pl.BlockSpec((1, D_CONV), const),
            pl.BlockSpec((1, D_CONV), const),
        ],
        out_specs=pl.BlockSpec((ts, D_CONV), lambda i: (i, 0)),
        out_shape=jax.ShapeDtypeStruct((s, D_CONV), bf16),
        scratch_shapes=[pltpu.VMEM((HALO + ts, D_CONV), f32)],
        compiler_params=_cparams(1),
        name="conv",
    )(hglu, hglu, w, b, lg, lb)


def _attn_body(q_ref, kt_ref, v_ref, lq1_ref, lk1_ref, lq2_ref, lk2_ref, sg_ref,
               o_ref, m_sc, l_sc, acc_sc):
    tq, tk = TQ_ATTN, TK_ATTN
    i = pl.program_id(1)
    q = q_ref[...]
    lane = lax.broadcasted_iota(jnp.int32, (tq, 2 * HEAD_DIM), 1)
    zero = jnp.zeros_like(q)
    qs = jnp.concatenate([jnp.where(lane < HEAD_DIM, q, zero),
                          jnp.where(lane >= HEAD_DIM, q, zero)], axis=0)

    m_sc[...] = jnp.full(m_sc.shape, NEG_BIG, f32)
    l_sc[...] = jnp.zeros(l_sc.shape, f32)
    acc_sc[...] = jnp.zeros(acc_sc.shape, f32)

    def block(kb, masked):
        s = jnp.dot(qs, kt_ref[kb], preferred_element_type=f32)
        if masked:
            row = lax.broadcasted_iota(jnp.int32, s.shape, 0)
            col = lax.broadcasted_iota(jnp.int32, s.shape, 1)
            vis = (col // CHUNK) <= ((row % tq) // CHUNK)
            s = jnp.where(vis, s, NEG_BIG)
        m_old = m_sc[...]
        m_new = jnp.maximum(m_old, jnp.max(s, axis=1, keepdims=True))
        alpha = jnp.exp(m_old - m_new)
        p = jnp.exp(s - m_new)
        l_sc[...] = alpha * l_sc[...] + jnp.sum(p, axis=1, keepdims=True)
        vb = v_ref[pl.ds(pl.multiple_of(kb * tk, tk), tk), :]
        acc_sc[...] = alpha * acc_sc[...] + jnp.dot(p.astype(bf16), vb, preferred_element_type=f32)
        m_sc[...] = m_new

    def full_block(kb, carry):
        block(kb, False)
        return carry

    lax.fori_loop(0, i, full_block, 0)
    block(i, True)

    lam = (jnp.exp(jnp.sum(lq1_ref[...] * lk1_ref[...], axis=-1, keepdims=True))
           - jnp.exp(jnp.sum(lq2_ref[...] * lk2_ref[...], axis=-1, keepdims=True))
           + LAMBDA_INIT)
    o_all = acc_sc[...] / l_sc[...]
    o = o_all[:tq] - lam * o_all[tq:]
    ms = jnp.mean(o * o, axis=-1, keepdims=True)
    y = o * lax.rsqrt(ms + EPS) * sg_ref[...] * (1.0 - LAMBDA_INIT)
    o_ref[...] = y.astype(bf16)


def _attn(q, kt, v, lq1, lk1, lq2, lk2, sg):
    s = q.shape[0]
    tq, tk = TQ_ATTN, TK_ATTN
    assert tq == tk and s % tq == 0
    vec = lambda h, i: (0, 0)
    return pl.pallas_call(
        _attn_body,
        grid=(N_HEADS, s // tq),
        in_specs=[
            pl.BlockSpec((tq, 2 * HEAD_DIM), lambda h, i: (i, h)),
            pl.BlockSpec((s // tk, 2 * HEAD_DIM, tk), lambda h, i: (0, h, 0)),
            pl.BlockSpec((s, V_DIM), lambda h, i: (0, h)),
            pl.BlockSpec((1, HEAD_DIM), vec),
            pl.BlockSpec((1, HEAD_DIM), vec),
            pl.BlockSpec((1, HEAD_DIM), vec),
            pl.BlockSpec((1, HEAD_DIM), vec),
            pl.BlockSpec((1, V_DIM), vec),
        ],
        out_specs=pl.BlockSpec((tq, V_DIM), lambda h, i: (i, h)),
        out_shape=jax.ShapeDtypeStruct((s, D_V), bf16),
        scratch_shapes=[
            pltpu.VMEM((2 * tq, 1), f32),
            pltpu.VMEM((2 * tq, 1), f32),
            pltpu.VMEM((2 * tq, V_DIM), f32),
        ],
        compiler_params=_cparams(2),
        name="attn",
    )(q, kt, v, lq1, lk1, lq2, lk2, sg)


def _outproj_body(yc_ref, ya_ref, x_ref, w_ref, g2_ref, wr_ref, br_ref, h_ref, r_ref):
    h = (jnp.dot(yc_ref[...], w_ref[0:D_CONV, :], preferred_element_type=f32)
         + jnp.dot(ya_ref[...], w_ref[D_CONV:, :], preferred_element_type=f32)
         + x_ref[...])
    h_ref[...] = h
    ms = jnp.mean(h * h, axis=-1, keepdims=True)
    t = h * lax.rsqrt(ms + EPS) * g2_ref[...]
    logits = jnp.dot(t, wr_ref[...], preferred_element_type=f32,
                     precision=lax.Precision.HIGHEST) + br_ref[...]

    lane = lax.broadcasted_iota(jnp.int32, logits.shape, 1)
    lanef = lane.astype(f32)
    big = float(LANES)
    is_g = lane < N_GROUPS
    gl = jnp.where(is_g, logits, NEG_BIG)
    gmax = jnp.max(gl, axis=1, keepdims=True)
    gidx = jnp.min(jnp.where(gl == gmax, lanef, big), axis=1, keepdims=True)
    gsum = jnp.sum(jnp.where(is_g, jnp.exp(gl - gmax), 0.0), axis=1, keepdims=True)
    gw = 1.0 / gsum

    e_of_lane = lane - N_GROUPS
    in_group = ((e_of_lane >= 0) & (e_of_lane < N_EXPERTS)
                & ((e_of_lane // EXPERTS_PER_GROUP).astype(f32) == gidx))
    el = jnp.where(in_group, logits, NEG_BIG)
    v1 = jnp.max(el, axis=1, keepdims=True)
    i1 = jnp.min(jnp.where(el == v1, lanef, big), axis=1, keepdims=True)
    el2 = jnp.where(lanef == i1, NEG_BIG, el)
    v2 = jnp.max(el2, axis=1, keepdims=True)
    i2 = jnp.min(jnp.where(el2 == v2, lanef, big), axis=1, keepdims=True)
    e21 = jnp.exp(v2 - v1)
    den = 1.0 + e21
    w1 = gw * (1.0 / den)
    w2 = gw * (e21 / den)
    r = jnp.where(lane == 0, i1 - N_GROUPS,
                  jnp.where(lane == 1, i2 - N_GROUPS,
                            jnp.where(lane == 2, w1, jnp.where(lane == 3, w2, 0.0))))
    r_ref[...] = r


def _outproj(yc, ya, x2, w_bf, g2, wr, br):
    s = x2.shape[0]
    tm = TM_PROJ
    const = lambda i: (0, 0)
    row = lambda i: (i, 0)
    return pl.pallas_call(
        _outproj_body,
        grid=(s // tm,),
        in_specs=[
            pl.BlockSpec((tm, D_CONV), row),
            pl.BlockSpec((tm, D_V), row),
            pl.BlockSpec((tm, D_MODEL), row),
            pl.BlockSpec((D_CONV + D_V, D_MODEL), const, pipeline_mode=pl.Buffered(1)),
            pl.BlockSpec((1, D_MODEL), const),
            pl.BlockSpec((D_MODEL, LANES), const),
            pl.BlockSpec((1, LANES), const),
        ],
        out_specs=[pl.BlockSpec((tm, D_MODEL), row), pl.BlockSpec((tm, LANES), row)],
        out_shape=[jax.ShapeDtypeStruct((s, D_MODEL), f32), jax.ShapeDtypeStruct((s, LANES), f32)],
        compiler_params=_cparams(1),
        name="outproj",
    )(yc, ya, x2, w_bf, g2, wr, br)


def _rank_body(r_ref, rank_ref, cnt_ref, run_sc):
    tb = TB_RANK
    i = pl.program_id(0)

    @pl.when(i == 0)
    def _():
        run_sc[...] = jnp.zeros(run_sc.shape, f32)

    r = r_ref[...]
    lane = lax.broadcasted_iota(jnp.int32, r.shape, 1)
    lanef = lane.astype(f32)
    oh1 = (lanef == r[:, 0:1]).astype(f32)
    oh2 = (lanef == r[:, 1:2]).astype(f32)
    both = oh1 + oh2
    row = lax.broadcasted_iota(jnp.int32, (tb, tb), 0)
    col = lax.broadcasted_iota(jnp.int32, (tb, tb), 1)
    tri = (row > col).astype(bf16)
    before = jnp.dot(tri, both.astype(bf16), preferred_element_type=f32) + run_sc[...]
    rank1 = jnp.sum(before * oh1, axis=1, keepdims=True)
    rank2 = jnp.sum(before * oh2, axis=1, keepdims=True)
    rank_ref[...] = jnp.where(lane == 0, rank1, jnp.where(lane == 1, rank2, 0.0))
    run_sc[...] = run_sc[...] + jnp.sum(both, axis=0, keepdims=True)
    cnt_ref[...] = run_sc[...]


def _rank(r):
    s = r.shape[0]
    tb = TB_RANK
    return pl.pallas_call(
        _rank_body,
        grid=(s // tb,),
        in_specs=[pl.BlockSpec((tb, LANES), lambda i: (i, 0))],
        out_specs=[pl.BlockSpec((tb, LANES), lambda i: (i, 0)),
                   pl.BlockSpec((1, LANES), lambda i: (0, 0))],
        out_shape=[jax.ShapeDtypeStruct((s, LANES), f32), jax.ShapeDtypeStruct((1, LANES), f32)],
        scratch_shapes=[pltpu.VMEM((1, LANES), f32)],
        compiler_params=_cparams(1),
        name="rank",
    )(r)


def _row_copy(src, dst, sem):
    return pltpu.make_async_copy(src, dst, sem)


def _dispatch_body(pos1_ref, pos2_ref, h_ref, g2_ref, xs_init_ref, xs_ref, tbuf, sems):
    del xs_init_ref
    tb = TB_ROWS
    i = pl.program_id(0)
    n = pl.num_programs(0)
    slot = i % 2

    def wait_slot(sl):
        for _ in range(2):
            _row_copy(tbuf.at[sl], xs_ref.at[pl.ds(0, tb)], sems.at[sl]).wait()

    @pl.when(i >= 2)
    def _():
        wait_slot(slot)

    h = h_ref[...]
    ms = jnp.mean(h * h, axis=-1, keepdims=True)
    tbuf[slot] = h * lax.rsqrt(ms + EPS) * g2_ref[...]

    def issue(r, carry):
        src = tbuf.at[slot, pl.ds(r, 1)]
        _row_copy(src, xs_ref.at[pl.ds(pos1_ref[r], 1)], sems.at[slot]).start()
        _row_copy(src, xs_ref.at[pl.ds(pos2_ref[r], 1)], sems.at[slot]).start()
        return carry

    lax.fori_loop(0, tb, issue, 0)

    @pl.when(i == n - 1)
    def _():
        wait_slot(slot)

        @pl.when(n >= 2)
        def _():
            wait_slot(1 - slot)


def _dispatch(pos1, pos2, h, g2, n_rows):
    s = h.shape[0]
    tb = TB_ROWS
    return pl.pallas_call(
        _dispatch_body,
        grid=(s // tb,),
        in_specs=[
            pl.BlockSpec((tb,), lambda i: (i,), memory_space=pltpu.SMEM),
            pl.BlockSpec((tb,), lambda i: (i,), memory_space=pltpu.SMEM),
            pl.BlockSpec((tb, D_MODEL), lambda i: (i, 0)),
            pl.BlockSpec((1, D_MODEL), lambda i: (0, 0)),
            pl.BlockSpec(memory_space=pl.ANY),
        ],
        out_specs=pl.BlockSpec(memory_space=pl.ANY),
        out_shape=jax.ShapeDtypeStruct((n_rows, D_MODEL), f32),
        input_output_aliases={4: 0},
        scratch_shapes=[pltpu.VMEM((2, tb, D_MODEL), f32), pltpu.SemaphoreType.DMA((2,))],
        compiler_params=_cparams(1),
        name="dispatch",
    )(pos1, pos2, h, g2, jnp.zeros((n_rows, D_MODEL), f32))


def _experts_body(te_ref, nv_ref, xs_ref, wg_ref, wu_ref, wd_ref, y_ref, wg_sc, wu_sc, wd_sc):
    i = pl.program_id(0)

    @pl.when(i < nv_ref[0])
    def _():
        prev = te_ref[jnp.maximum(i - 1, 0)]

        @pl.when((i == 0) | (te_ref[i] != prev))
        def _():
            wg_sc[...] = wg_ref[0].astype(bf16)
            wu_sc[...] = wu_ref[0].astype(bf16)
            wd_sc[...] = wd_ref[0].astype(bf16)

        x = xs_ref[...].astype(bf16)
        g = jnp.dot(x, wg_sc[...], preferred_element_type=f32)
        u = jnp.dot(x, wu_sc[...], preferred_element_type=f32)
        hm = (g * jax.nn.sigmoid(g) * u).astype(bf16)
        y_ref[...] = jnp.dot(hm, wd_sc[...], preferred_element_type=f32)

    @pl.when(i >= nv_ref[0])
    def _():
        y_ref[...] = jnp.zeros(y_ref.shape, f32)


def _experts(tile_expert, n_valid, xs, w_gate, w_up, w_down):
    n_rows = xs.shape[0]
    tm = TM_EXP
    nt = n_rows // tm
    rows = lambda i, te, nv: (jnp.minimum(i, nv[0] - 1), 0)
    wsel = lambda i, te, nv: (te[i], 0, 0)
    grid_spec = pltpu.PrefetchScalarGridSpec(
        num_scalar_prefetch=2,
        grid=(nt,),
        in_specs=[
            pl.BlockSpec((tm, D_MODEL), rows),
            pl.BlockSpec((1, D_MODEL, D_EXPERT), wsel),
            pl.BlockSpec((1, D_MODEL, D_EXPERT), wsel),
            pl.BlockSpec((1, D_EXPERT, D_MODEL), wsel),
        ],
        out_specs=pl.BlockSpec((tm, D_MODEL), lambda i, te, nv: (i, 0)),
        scratch_shapes=[
            pltpu.VMEM((D_MODEL, D_EXPERT), bf16),
            pltpu.VMEM((D_MODEL, D_EXPERT), bf16),
            pltpu.VMEM((D_EXPERT, D_MODEL), bf16),
        ],
    )
    return pl.pallas_call(
        _experts_body,
        grid_spec=grid_spec,
        out_shape=jax.ShapeDtypeStruct((n_rows, D_MODEL), f32),
        compiler_params=_cparams(1),
        name="experts",
    )(tile_expert, n_valid, xs, w_gate, w_up, w_down)


def _combine_body(p1_ref, p2_ref, p1n_ref, p2n_ref, h_ref, r_ref, y_ref, o_ref, ybuf, sems):
    tb = TB_ROWS
    i = pl.program_id(0)
    n = pl.num_programs(0)
    slot = i % 2

    def issue_block(pa_ref, pb_ref, sl):
        def issue(r, carry):
            _row_copy(y_ref.at[pl.ds(pa_ref[r], 1)], ybuf.at[sl, 0, pl.ds(r, 1)], sems.at[sl]).start()
            _row_copy(y_ref.at[pl.ds(pb_ref[r], 1)], ybuf.at[sl, 1, pl.ds(r, 1)], sems.at[sl]).start()
            return carry
        lax.fori_loop(0, tb, issue, 0)

    @pl.when(i == 0)
    def _():
        issue_block(p1_ref, p2_ref, 0)

    @pl.when(i + 1 < n)
    def _():
        issue_block(p1n_ref, p2n_ref, 1 - slot)

    for k in range(2):
        _row_copy(y_ref.at[pl.ds(0, tb)], ybuf.at[slot, k], sems.at[slot]).wait()

    r = r_ref[...]
    o_ref[...] = h_ref[...] + r[:, 2:3] * ybuf[slot, 0] + r[:, 3:4] * ybuf[slot, 1]


def _combine(pos1, pos2, h, r, y):
    s = h.shape[0]
    tb = TB_ROWS
    nb = s // tb
    cur = lambda i: (i,)
    nxt = lambda i: (jnp.minimum(i + 1, nb - 1),)
    smem = functools.partial(pl.BlockSpec, (tb,), memory_space=pltpu.SMEM)
    return pl.pallas_call(
        _combine_body,
        grid=(nb,),
        in_specs=[
            smem(cur), smem(cur), smem(nxt), smem(nxt),
            pl.BlockSpec((tb, D_MODEL), lambda i: (i, 0)),
            pl.BlockSpec((tb, LANES), lambda i: (i, 0)),
            pl.BlockSpec(memory_space=pl.ANY),
        ],
        out_specs=pl.BlockSpec((tb, D_MODEL), lambda i: (i, 0)),
        out_shape=jax.ShapeDtypeStruct((s, D_MODEL), f32),
        scratch_shapes=[pltpu.VMEM((2, 2, tb, D_MODEL), f32), pltpu.SemaphoreType.DMA((2,))],
        compiler_params=_cparams(1),
        name="combine",
    )(pos1, pos2, pos1, pos2, h, r, y)


def kernel(x, norm1_g, w_in, conv_dw_kernel, conv_dw_bias, conv_ln_g, conv_ln_b, q_norm_g, k_norm_g,
           lambda_q1, lambda_k1, lambda_q2, lambda_k2, subln_g, w_out, norm2_g, w_group, b_group,
           w_router, b_router, w_gate, w_up, w_down):
    b, s, d = x.shape
    assert b == 1 and d == D_MODEL and norm1_g.shape[0] == 1
    x2 = x.reshape(s, d)

    w_in_bf = w_in[0].astype(bf16)
    w_out_bf = w_out[0].astype(bf16)
    qg = jnp.tile(q_norm_g[0].reshape(1, 2 * HEAD_DIM), (1, N_HEADS)) * (HEAD_DIM ** -0.5)
    kg = jnp.tile(k_norm_g[0].reshape(1, 2 * HEAD_DIM), (1, N_HEADS))
    blk = jnp.arange(256) // HEAD_DIM
    bd = jnp.where(blk[:, None] == blk[None, :], 1.0 / HEAD_DIM, 0.0).astype(bf16)
    n_r = N_GROUPS + N_EXPERTS
    wr = jnp.concatenate(
        [w_group[0], jnp.transpose(w_router[0], (1, 0, 2)).reshape(d, N_EXPERTS),
         jnp.zeros((d, LANES - n_r), f32)], axis=1)
    br = jnp.concatenate([b_group[0], b_router[0].reshape(N_EXPERTS), jnp.zeros((LANES - n_r,), f32)])[None]
    row = lambda a: a.reshape(1, -1)

    hglu, q, kt, v = _inproj(x2, norm1_g, w_in_bf, qg, kg, bd)
    y_conv = _conv(hglu, conv_dw_kernel[0], row(conv_dw_bias[0]), row(conv_ln_g[0]), row(conv_ln_b[0]))
    y_attn = _attn(q, kt, v, row(lambda_q1[0]), row(lambda_k1[0]), row(lambda_q2[0]), row(lambda_k2[0]),
                   row(subln_g[0]))
    h, r = _outproj(y_conv, y_attn, x2, w_out_bf, norm2_g, wr, br)

    rank, cnt = _rank(r)
    tm = TM_EXP
    counts = cnt[0, :N_EXPERTS].astype(jnp.int32)
    padded = ((counts + tm - 1) // tm) * tm
    ends = jnp.cumsum(padded)
    offsets = ends - padded
    e1 = r[:, 0].astype(jnp.int32)
    e2 = r[:, 1].astype(jnp.int32)
    pos1 = offsets[e1] + rank[:, 0].astype(jnp.int32)
    pos2 = offsets[e2] + rank[:, 1].astype(jnp.int32)
    n_rows = TOP_K * s + N_EXPERTS * tm
    nt = n_rows // tm
    n_valid = (ends[-1] // tm).astype(jnp.int32)
    tile_start = jnp.minimum(jnp.arange(nt, dtype=jnp.int32), n_valid - 1) * tm
    tile_expert = jnp.sum(tile_start[:, None] >= ends[None, :], axis=1).astype(jnp.int32)

    xs = _dispatch(pos1, pos2, h, norm2_g, n_rows)
    ys = _experts(tile_expert, n_valid.reshape(1), xs, w_gate[0], w_up[0], w_down[0])
    out = _combine(pos1, pos2, h, r, ys)
    return out.reshape(b, s, d)
```

```python
import functools
import math

import jax
import jax.numpy as jnp
from jax import lax
from jax.experimental import pallas as pl
from jax.experimental.pallas import tpu as pltpu

D_MODEL = 2048
D_CONV = 1024
N_HEADS = 8
HEAD_DIM = 64
V_DIM = 128
CONV_WIDTH = 31
CHUNK = 64
N_GROUPS = 4
EXPERTS_PER_GROUP = 8
N_EXPERTS = 32
TOP_K = 2
D_EXPERT = 512
D_Q = 1024
D_V = 1024
D_IN = 2 * D_CONV + 2 * D_Q + D_V
EPS = 1e-6
LAMBDA_INIT = 0.8 - 0.6 * math.exp(-0.3 * 0)

LANES = 128
VMEM_LIMIT = 56 * 1024 * 1024
NEG_BIG = -1e30

TM_PROJ = 512
TK_ATTN = 512
TQ_ATTN = 512
TS_CONV = 128
RC_CONV = 16
HALO = 32
TB_RANK = 512
TB_ROWS = 256
TM_EXP = 256

bf16 = jnp.bfloat16
f32 = jnp.float32


def _cparams(n_axes):
    return pltpu.CompilerParams(
        dimension_semantics=("arbitrary",) * n_axes, vmem_limit_bytes=VMEM_LIMIT)


def _inproj_body(x_ref, g1_ref, w_ref, qg_ref, kg_ref, bd_ref,
                 hglu_ref, qt_ref, k_ref, vt_ref):
    x = x_ref[...]
    ms = jnp.mean(x * x, axis=-1, keepdims=True)
    xn = (x * lax.rsqrt(ms + EPS) * g1_ref[...]).astype(bf16)

    def mm(c0, n):
        return jnp.dot(xn, w_ref[:, c0:c0 + n], preferred_element_type=f32)

    nt = 512
    for c in range(0, D_CONV, nt):
        a = mm(c, nt)
        b = mm(D_CONV + c, nt)
        hglu_ref[:, c:c + nt] = a * jax.nn.sigmoid(b)

    bd = bd_ref[...]
    sl = 256

    def qk_norm(c0, gain):
        u = mm(c0, sl)
        ms64 = jnp.dot((u * u).astype(bf16), bd, preferred_element_type=f32)
        return u * lax.rsqrt(ms64 + EPS) * gain

    for c in range(0, D_Q, sl):
        qn = qk_norm(2 * D_CONV + c, qg_ref[:, c:c + sl])
        qt_ref[c:c + sl, :] = qn.T.astype(bf16)
    for c in range(0, D_Q, sl):
        k_ref[:, c:c + sl] = qk_norm(2 * D_CONV + D_Q + c, kg_ref[:, c:c + sl]).astype(bf16)
    for c in range(0, D_V, sl):
        vt_ref[0, c:c + sl, :] = mm(2 * D_CONV + 2 * D_Q + c, sl).T.astype(bf16)


def _inproj(x2, g1, w_bf, qg, kg, bd):
    s = x2.shape[0]
    tm = TM_PROJ
    assert tm == TK_ATTN and s % tm == 0
    const = lambda i: (0, 0)
    return pl.pallas_call(
        _inproj_body,
        grid=(s // tm,),
        in_specs=[
            pl.BlockSpec((tm, D_MODEL), lambda i: (i, 0)),
            pl.BlockSpec((1, D_MODEL), const),
            pl.BlockSpec((D_MODEL, D_IN), const, pipeline_mode=pl.Buffered(1)),
            pl.BlockSpec((1, D_Q), const),
            pl.BlockSpec((1, D_Q), const),
            pl.BlockSpec((256, 256), const),
        ],
        out_specs=[
            pl.BlockSpec((tm, D_CONV), lambda i: (i, 0)),
            pl.BlockSpec((D_Q, tm), lambda i: (0, i)),
            pl.BlockSpec((tm, D_Q), lambda i: (i, 0)),
            pl.BlockSpec((1, D_V, tm), lambda i: (i, 0, 0)),
        ],
        out_shape=[
            jax.ShapeDtypeStruct((s, D_CONV), f32),
            jax.ShapeDtypeStruct((D_Q, s), bf16),
            jax.ShapeDtypeStruct((s, D_Q), bf16),
            jax.ShapeDtypeStruct((s // tm, D_V, tm), bf16),
        ],
        compiler_params=_cparams(1),
        name="inproj",
    )(x2, g1, w_bf, qg, kg, bd)


def _conv_body(prev_ref, cur_ref, w_ref, b_ref, lg_ref, lb_ref, o_ref, pad_ref):
    i = pl.program_id(0)
    pad_ref[0:HALO, :] = jnp.where(i > 0, prev_ref[...], 0.0)
    pad_ref[HALO:HALO + TS_CONV, :] = cur_ref[...]
    off = HALO - (CONV_WIDTH - 1)
    for r0 in range(0, TS_CONV, RC_CONV):
        acc = jnp.broadcast_to(b_ref[...], (RC_CONV, D_CONV))
        for j in range(CONV_WIDTH):
            acc = acc + w_ref[j:j + 1, :] * pad_ref[r0 + off + j:r0 + off + j + RC_CONV, :]
        mu = jnp.mean(acc, axis=-1, keepdims=True)
        xc = acc - mu
        var = jnp.mean(xc * xc, axis=-1, keepdims=True)
        y = xc * lax.rsqrt(var + EPS) * lg_ref[...] + lb_ref[...]
        o_ref[r0:r0 + RC_CONV, :] = (y * jax.nn.sigmoid(y)).astype(bf16)


def _conv(hglu, w, b, lg, lb):
    s = hglu.shape[0]
    ts = TS_CONV
    const = lambda i: (0, 0)
    return pl.pallas_call(
        _conv_body,
        grid=(s // ts,),
        in_specs=[
            pl.BlockSpec((HALO, D_CONV), lambda i: (jnp.maximum(i * (ts // HALO) - 1, 0), 0)),
            pl.BlockSpec((ts, D_CONV), lambda i: (i, 0)),
            pl.BlockSpec((CONV_WIDTH, D_CONV), const),
            pl.BlockSpec((1, D_CONV), const),
            pl.BlockSpec((1, D_CONV), const),
            pl.BlockSpec((1, D_CONV), const),
        ],
        out_specs=pl.BlockSpec((ts, D_CONV), lambda i: (i, 0)),
        out_shape=jax.ShapeDtypeStruct((s, D_CONV), bf16),
        scratch_shapes=[pltpu.VMEM((HALO + ts, D_CONV), f32)],
        compiler_params=_cparams(1),
        name="conv",
    )(hglu, hglu, w, b, lg, lb)


def _attn_body(qt_ref, k_ref, vt_ref, lq1_ref, lk1_ref, lq2_ref, lk2_ref, sg_ref,
               o_ref, m_sc, l_sc, acc_sc, sa_sc, sb_sc):
    tq, tk = TQ_ATTN, TK_ATTN
    i = pl.program_id(1)
    qt = qt_ref[...]
    sub = lax.broadcasted_iota(jnp.int32, qt.shape, 0)
    zero = jnp.zeros_like(qt)
    qst = jnp.concatenate([jnp.where(sub < HEAD_DIM, qt, zero),
                           jnp.where(sub >= HEAD_DIM, qt, zero)], axis=1)

    m_sc[...] = jnp.full(m_sc.shape, NEG_BIG, f32)
    l_sc[...] = jnp.zeros(l_sc.shape, f32)
    acc_sc[...] = jnp.zeros(acc_sc.shape, f32)

    def scores(kb, dst):
        kblk = k_ref[pl.ds(pl.multiple_of(kb * tk, tk), tk), :]
        dst[...] = jnp.dot(kblk, qst, preferred_element_type=f32)

    def consume(src, kb, masked):
        s = src[...]
        if masked:
            key = lax.broadcasted_iota(jnp.int32, s.shape, 0)
            qry = lax.broadcasted_iota(jnp.int32, s.shape, 1)
            vis = (key // CHUNK) <= ((qry % tq) // CHUNK)
            s = jnp.where(vis, s, NEG_BIG)
        m_old = m_sc[...]
        m_new = jnp.maximum(m_old, jnp.max(s, axis=0, keepdims=True))
        alpha = jnp.exp2(m_old - m_new)
        p = jnp.exp2(s - m_new)
        l_sc[...] = alpha * l_sc[...] + jnp.sum(p, axis=0, keepdims=True)
        acc_sc[...] = alpha * acc_sc[...] + jnp.dot(vt_ref[kb], p.astype(bf16),
                                                    preferred_element_type=f32)
        m_sc[...] = m_new

    last_full = jnp.maximum(i - 1, 0)
    scores(i, sa_sc)
    scores(0, sb_sc)
    consume(sa_sc, i, True)

    def pair(j, carry):
        scores(2 * j + 1, sa_sc)
        consume(sb_sc, 2 * j, False)
        scores(jnp.minimum(2 * j + 2, last_full), sb_sc)
        consume(sa_sc, 2 * j + 1, False)
        return carry

    lax.fori_loop(0, i // 2, pair, 0)

    @pl.when(i % 2 == 1)
    def _():
        consume(sb_sc, i - 1, False)

    lam = (jnp.exp(jnp.sum(lq1_ref[...] * lk1_ref[...], axis=-1, keepdims=True))
           - jnp.exp(jnp.sum(lq2_ref[...] * lk2_ref[...], axis=-1, keepdims=True))
           + LAMBDA_INIT)
    o_all = acc_sc[...] / l_sc[...]
    o = (o_all[:, :tq] - lam * o_all[:, tq:]).T
    ms = jnp.mean(o * o, axis=-1, keepdims=True)
    y = o * lax.rsqrt(ms + EPS) * sg_ref[...] * (1.0 - LAMBDA_INIT)
    o_ref[...] = y.astype(bf16)


def _attn(qt, k, vt, lq1, lk1, lq2, lk2, sg):
    s = k.shape[0]
    tq, tk = TQ_ATTN, TK_ATTN
    assert tq == tk and s % tq == 0
    vec = lambda h, i: (0, 0)
    return pl.pallas_call(
        _attn_body,
        grid=(N_HEADS, s // tq),
        in_specs=[
            pl.BlockSpec((2 * HEAD_DIM, tq), lambda h, i: (h, i)),
            pl.BlockSpec((s, 2 * HEAD_DIM), lambda h, i: (0, h)),
            pl.BlockSpec((s // tk, V_DIM, tk), lambda h, i: (0, h, 0)),
            pl.BlockSpec((1, HEAD_DIM), vec),
            pl.BlockSpec((1, HEAD_DIM), vec),
            pl.BlockSpec((1, HEAD_DIM), vec),
            pl.BlockSpec((1, HEAD_DIM), vec),
            pl.BlockSpec((1, V_DIM), vec),
        ],
        out_specs=pl.BlockSpec((tq, V_DIM), lambda h, i: (i, h)),
        out_shape=jax.ShapeDtypeStruct((s, D_V), bf16),
        scratch_shapes=[
            pltpu.VMEM((1, 2 * tq), f32),
            pltpu.VMEM((1, 2 * tq), f32),
            pltpu.VMEM((V_DIM, 2 * tq), f32),
            pltpu.VMEM((tk, 2 * tq), f32),
            pltpu.VMEM((tk, 2 * tq), f32),
        ],
        compiler_params=_cparams(2),
        name="attn",
    )(qt, k, vt, lq1, lk1, lq2, lk2, sg)


def _outproj_body(yc_ref, ya_ref, x_ref, w_ref, g2_ref, wr_ref, br_ref, h_ref, r_ref):
    h = (jnp.dot(yc_ref[...], w_ref[0:D_CONV, :], preferred_element_type=f32)
         + jnp.dot(ya_ref[...], w_ref[D_CONV:, :], preferred_element_type=f32)
         + x_ref[...])
    h_ref[...] = h
    ms = jnp.mean(h * h, axis=-1, keepdims=True)
    t = h * lax.rsqrt(ms + EPS) * g2_ref[...]
    logits = jnp.dot(t, wr_ref[...], preferred_element_type=f32,
                     precision=lax.Precision.HIGHEST) + br_ref[...]

    lane = lax.broadcasted_iota(jnp.int32, logits.shape, 1)
    lanef = lane.astype(f32)
    big = float(LANES)
    is_g = lane < N_GROUPS
    gl = jnp.where(is_g, logits, NEG_BIG)
    gmax = jnp.max(gl, axis=1, keepdims=True)
    gidx = jnp.min(jnp.where(gl == gmax, lanef, big), axis=1, keepdims=True)
    gsum = jnp.sum(jnp.where(is_g, jnp.exp(gl - gmax), 0.0), axis=1, keepdims=True)
    gw = 1.0 / gsum

    e_of_lane = lane - N_GROUPS
    in_group = ((e_of_lane >= 0) & (e_of_lane < N_EXPERTS)
                & ((e_of_lane // EXPERTS_PER_GROUP).astype(f32) == gidx))
    el = jnp.where(in_group, logits, NEG_BIG)
    v1 = jnp.max(el, axis=1, keepdims=True)
    i1 = jnp.min(jnp.where(el == v1, lanef, big), axis=1, keepdims=True)
    el2 = jnp.where(lanef == i1, NEG_BIG, el)
    v2 = jnp.max(el2, axis=1, keepdims=True)
    i2 = jnp.min(jnp.where(el2 == v2, lanef, big), axis=1, keepdims=True)
    e21 = jnp.exp(v2 - v1)
    den = 1.0 + e21
    w1 = gw * (1.0 / den)
    w2 = gw * (e21 / den)
    r = jnp.where(lane == 0, i1 - N_GROUPS,
                  jnp.where(lane == 1, i2 - N_GROUPS,
                            jnp.where(lane == 2, w1, jnp.where(lane == 3, w2, 0.0))))
    r_ref[...] = r


def _outproj(yc, ya, x2, w_bf, g2, wr, br):
    s = x2.shape[0]
    tm = TM_PROJ
    const = lambda i: (0, 0)
    row = lambda i: (i, 0)
    return pl.pallas_call(
        _outproj_body,
        grid=(s // tm,),
        in_specs=[
            pl.BlockSpec((tm, D_CONV), row),
            pl.BlockSpec((tm, D_V), row),
            pl.BlockSpec((tm, D_MODEL), row),
            pl.BlockSpec((D_CONV + D_V, D_MODEL), const, pipeline_mode=pl.Buffered(1)),
            pl.BlockSpec((1, D_MODEL), const),
            pl.BlockSpec((D_MODEL, LANES), const),
            pl.BlockSpec((1, LANES), const),
        ],
        out_specs=[pl.BlockSpec((tm, D_MODEL), row), pl.BlockSpec((tm, LANES), row)],
        out_shape=[jax.ShapeDtypeStruct((s, D_MODEL), f32), jax.ShapeDtypeStruct((s, LANES), f32)],
        compiler_params=_cparams(1),
        name="outproj",
    )(yc, ya, x2, w_bf, g2, wr, br)


def _rank_body(r_ref, rank_ref, cnt_ref, run_sc):
    tb = TB_RANK
    i = pl.program_id(0)

    @pl.when(i == 0)
    def _():
        run_sc[...] = jnp.zeros(run_sc.shape, f32)

    r = r_ref[...]
    lane = lax.broadcasted_iota(jnp.int32, r.shape, 1)
    lanef = lane.astype(f32)
    oh1 = (lanef == r[:, 0:1]).astype(f32)
    oh2 = (lanef == r[:, 1:2]).astype(f32)
    both = oh1 + oh2
    row = lax.broadcasted_iota(jnp.int32, (tb, tb), 0)
    col = lax.broadcasted_iota(jnp.int32, (tb, tb), 1)
    tri = (row > col).astype(bf16)
    before = jnp.dot(tri, both.astype(bf16), preferred_element_type=f32) + run_sc[...]
    rank1 = jnp.sum(before * oh1, axis=1, keepdims=True)
    rank2 = jnp.sum(before * oh2, axis=1, keepdims=True)
    rank_ref[...] = jnp.where(lane == 0, rank1, jnp.where(lane == 1, rank2, 0.0))
    run_sc[...] = run_sc[...] + jnp.sum(both, axis=0, keepdims=True)
    cnt_ref[...] = run_sc[...]


def _rank(r):
    s = r.shape[0]
    tb = TB_RANK
    return pl.pallas_call(
        _rank_body,
        grid=(s // tb,),
        in_specs=[pl.BlockSpec((tb, LANES), lambda i: (i, 0))],
        out_specs=[pl.BlockSpec((tb, LANES), lambda i: (i, 0)),
                   pl.BlockSpec((1, LANES), lambda i: (0, 0))],
        out_shape=[jax.ShapeDtypeStruct((s, LANES), f32), jax.ShapeDtypeStruct((1, LANES), f32)],
        scratch_shapes=[pltpu.VMEM((1, LANES), f32)],
        compiler_params=_cparams(1),
        name="rank",
    )(r)


def _row_copy(src, dst, sem):
    return pltpu.make_async_copy(src, dst, sem)


def _dispatch_body(pos1_ref, pos2_ref, h_ref, g2_ref, xs_init_ref, xs_ref, tbuf, sems):
    del xs_init_ref
    tb = TB_ROWS
    i = pl.program_id(0)
    n = pl.num_programs(0)
    slot = i % 2

    def wait_slot(sl):
        for _ in range(2):
            _row_copy(tbuf.at[sl], xs_ref.at[pl.ds(0, tb)], sems.at[sl]).wait()

    @pl.when(i >= 2)
    def _():
        wait_slot(slot)

    h = h_ref[...]
    ms = jnp.mean(h * h, axis=-1, keepdims=True)
    tbuf[slot] = h * lax.rsqrt(ms + EPS) * g2_ref[...]

    def issue(r, carry):
        src = tbuf.at[slot, pl.ds(r, 1)]
        _row_copy(src, xs_ref.at[pl.ds(pos1_ref[r], 1)], sems.at[slot]).start()
        _row_copy(src, xs_ref.at[pl.ds(pos2_ref[r], 1)], sems.at[slot]).start()
        return carry

    lax.fori_loop(0, tb, issue, 0)

    @pl.when(i == n - 1)
    def _():
        wait_slot(slot)

        @pl.when(n >= 2)
        def _():
            wait_slot(1 - slot)


def _dispatch(pos1, pos2, h, g2, n_rows):
    s = h.shape[0]
    tb = TB_ROWS
    return pl.pallas_call(
        _dispatch_body,
        grid=(s // tb,),
        in_specs=[
            pl.BlockSpec((tb,), lambda i: (i,), memory_space=pltpu.SMEM),
            pl.BlockSpec((tb,), lambda i: (i,), memory_space=pltpu.SMEM),
            pl.BlockSpec((tb, D_MODEL), lambda i: (i, 0)),
            pl.BlockSpec((1, D_MODEL), lambda i: (0, 0)),
            pl.BlockSpec(memory_space=pl.ANY),
        ],
        out_specs=pl.BlockSpec(memory_space=pl.ANY),
        out_shape=jax.ShapeDtypeStruct((n_rows, D_MODEL), f32),
        input_output_aliases={4: 0},
        scratch_shapes=[pltpu.VMEM((2, tb, D_MODEL), f32), pltpu.SemaphoreType.DMA((2,))],
        compiler_params=_cparams(1),
        name="dispatch",
    )(pos1, pos2, h, g2, jnp.zeros((n_rows, D_MODEL), f32))


def _experts_body(te_ref, nv_ref, xs_ref, wg_ref, wu_ref, wd_ref, y_ref, wg_sc, wu_sc, wd_sc):
    i = pl.program_id(0)

    @pl.when(i < nv_ref[0])
    def _():
        prev = te_ref[jnp.maximum(i - 1, 0)]

        @pl.when((i == 0) | (te_ref[i] != prev))
        def _():
            wg_sc[...] = wg_ref[0].astype(bf16)
            wu_sc[...] = wu_ref[0].astype(bf16)
            wd_sc[...] = wd_ref[0].astype(bf16)

        x = xs_ref[...].astype(bf16)
        g = jnp.dot(x, wg_sc[...], preferred_element_type=f32)
        u = jnp.dot(x, wu_sc[...], preferred_element_type=f32)
        hm = (g * jax.nn.sigmoid(g) * u).astype(bf16)
        y_ref[...] = jnp.dot(hm, wd_sc[...], preferred_element_type=f32)

    @pl.when(i >= nv_ref[0])
    def _():
        y_ref[...] = jnp.zeros(y_ref.shape, f32)


def _experts(tile_expert, n_valid, xs, w_gate, w_up, w_down):
    n_rows = xs.shape[0]
    tm = TM_EXP
    nt = n_rows // tm
    rows = lambda i, te, nv: (jnp.minimum(i, nv[0] - 1), 0)
    wsel = lambda i, te, nv: (te[i], 0, 0)
    grid_spec = pltpu.PrefetchScalarGridSpec(
        num_scalar_prefetch=2,
        grid=(nt,),
        in_specs=[
            pl.BlockSpec((tm, D_MODEL), rows),
            pl.BlockSpec((1, D_MODEL, D_EXPERT), wsel),
            pl.BlockSpec((1, D_MODEL, D_EXPERT), wsel),
            pl.BlockSpec((1, D_EXPERT, D_MODEL), wsel),
        ],
        out_specs=pl.BlockSpec((tm, D_MODEL), lambda i, te, nv: (i, 0)),
        scratch_shapes=[
            pltpu.VMEM((D_MODEL, D_EXPERT), bf16),
            pltpu.VMEM((D_MODEL, D_EXPERT), bf16),
            pltpu.VMEM((D_EXPERT, D_MODEL), bf16),
        ],
    )
    return pl.pallas_call(
        _experts_body,
        grid_spec=grid_spec,
        out_shape=jax.ShapeDtypeStruct((n_rows, D_MODEL), f32),
        compiler_params=_cparams(1),
        name="experts",
    )(tile_expert, n_valid, xs, w_gate, w_up, w_down)


def _combine_body(p1_ref, p2_ref, p1n_ref, p2n_ref, h_ref, r_ref, y_ref, o_ref, ybuf, sems):
    tb = TB_ROWS
    i = pl.program_id(0)
    n = pl.num_programs(0)
    slot = i % 2

    def issue_block(pa_ref, pb_ref, sl):
        def issue(r, carry):
            _row_copy(y_ref.at[pl.ds(pa_ref[r], 1)], ybuf.at[sl, 0, pl.ds(r, 1)], sems.at[sl]).start()
            _row_copy(y_ref.at[pl.ds(pb_ref[r], 1)], ybuf.at[sl, 1, pl.ds(r, 1)], sems.at[sl]).start()
            return carry
        lax.fori_loop(0, tb, issue, 0)

    @pl.when(i == 0)
    def _():
        issue_block(p1_ref, p2_ref, 0)

    @pl.when(i + 1 < n)
    def _():
        issue_block(p1n_ref, p2n_ref, 1 - slot)

    for k in range(2):
        _row_copy(y_ref.at[pl.ds(0, tb)], ybuf.at[slot, k], sems.at[slot]).wait()

    r = r_ref[...]
    o_ref[...] = h_ref[...] + r[:, 2:3] * ybuf[slot, 0] + r[:, 3:4] * ybuf[slot, 1]


def _combine(pos1, pos2, h, r, y):
    s = h.shape[0]
    tb = TB_ROWS
    nb = s // tb
    cur = lambda i: (i,)
    nxt = lambda i: (jnp.minimum(i + 1, nb - 1),)
    smem = functools.partial(pl.BlockSpec, (tb,), memory_space=pltpu.SMEM)
    return pl.pallas_call(
        _combine_body,
        grid=(nb,),
        in_specs=[
            smem(cur), smem(cur), smem(nxt), smem(nxt),
            pl.BlockSpec((tb, D_MODEL), lambda i: (i, 0)),
            pl.BlockSpec((tb, LANES), lambda i: (i, 0)),
            pl.BlockSpec(memory_space=pl.ANY),
        ],
        out_specs=pl.BlockSpec((tb, D_MODEL), lambda i: (i, 0)),
        out_shape=jax.ShapeDtypeStruct((s, D_MODEL), f32),
        scratch_shapes=[pltpu.VMEM((2, 2, tb, D_MODEL), f32), pltpu.SemaphoreType.DMA((2,))],
        compiler_params=_cparams(1),
        name="combine",
    )(pos1, pos2, pos1, pos2, h, r, y)


def kernel(x, norm1_g, w_in, conv_dw_kernel, conv_dw_bias, conv_ln_g, conv_ln_b, q_norm_g, k_norm_g,
           lambda_q1, lambda_k1, lambda_q2, lambda_k2, subln_g, w_out, norm2_g, w_group, b_group,
           w_router, b_router, w_gate, w_up, w_down):
    b, s, d = x.shape
    assert b == 1 and d == D_MODEL and norm1_g.shape[0] == 1
    x2 = x.reshape(s, d)

    w_in_bf = w_in[0].astype(bf16)
    w_out_bf = w_out[0].astype(bf16)
    qg = jnp.tile(q_norm_g[0].reshape(1, 2 * HEAD_DIM), (1, N_HEADS)) * (HEAD_DIM ** -0.5 * math.log2(math.e))
    kg = jnp.tile(k_norm_g[0].reshape(1, 2 * HEAD_DIM), (1, N_HEADS))
    blk = jnp.arange(256) // HEAD_DIM
    bd = jnp.where(blk[:, None] == blk[None, :], 1.0 / HEAD_DIM, 0.0).astype(bf16)
    n_r = N_GROUPS + N_EXPERTS
    wr = jnp.concatenate(
        [w_group[0], jnp.transpose(w_router[0], (1, 0, 2)).reshape(d, N_EXPERTS),
         jnp.zeros((d, LANES - n_r), f32)], axis=1)
    br = jnp.concatenate([b_group[0], b_router[0].reshape(N_EXPERTS), jnp.zeros((LANES - n_r,), f32)])[None]
    row = lambda a: a.reshape(1, -1)

    hglu, qt, k, vt = _inproj(x2, norm1_g, w_in_bf, qg, kg, bd)
    y_conv = _conv(hglu, conv_dw_kernel[0], row(conv_dw_bias[0]), row(conv_ln_g[0]), row(conv_ln_b[0]))
    y_attn = _attn(qt, k, vt, row(lambda_q1[0]), row(lambda_k1[0]), row(lambda_q2[0]), row(lambda_k2[0]),
                   row(subln_g[0]))
    h, r = _outproj(y_conv, y_attn, x2, w_out_bf, norm2_g, wr, br)

    rank, cnt = _rank(r)
    tm = TM_EXP
    counts = cnt[0, :N_EXPERTS].astype(jnp.int32)
    padded = ((counts + tm - 1) // tm) * tm
    ends = jnp.cumsum(padded)
    offsets = ends - padded
    e1 = r[:, 0].astype(jnp.int32)
    e2 = r[:, 1].astype(jnp.int32)
    pos1 = offsets[e1] + rank[:, 0].astype(jnp.int32)
    pos2 = offsets[e2] + rank[:, 1].astype(jnp.int32)
    n_rows = TOP_K * s + N_EXPERTS * tm
    nt = n_rows // tm
    n_valid = (ends[-1] // tm).astype(jnp.int32)
    tile_start = jnp.minimum(jnp.arange(nt, dtype=jnp.int32), n_valid - 1) * tm
    tile_expert = jnp.sum(tile_start[:, None] >= ends[None, :], axis=1).astype(jnp.int32)

    xs = _dispatch(pos1, pos2, h, norm2_g, n_rows)
    ys = _experts(tile_expert, n_valid.reshape(1), xs, w_gate[0], w_up[0], w_down[0])
    out = _combine(pos1, pos2, h, r, ys)
    return out.reshape(b, s, d)
```

```python
import functools
import math

import jax
import jax.numpy as jnp
from jax import lax
from jax.experimental import pallas as pl
from jax.experimental.pallas import tpu as pltpu

D_MODEL = 2048
D_CONV = 1024
N_HEADS = 8
HEAD_DIM = 64
V_DIM = 128
CONV_WIDTH = 31
CHUNK = 64
N_GROUPS = 4
EXPERTS_PER_GROUP = 8
N_EXPERTS = 32
TOP_K = 2
D_EXPERT = 512
D_Q = 1024
D_V = 1024
D_IN = 2 * D_CONV + 2 * D_Q + D_V
EPS = 1e-6
LAMBDA_INIT = 0.8 - 0.6 * math.exp(-0.3 * 0)

LANES = 128
SUBLANES = 8
VMEM_LIMIT = 56 * 1024 * 1024
NEG_BIG = -1e30

TM_PROJ = 512
TK_ATTN = 512
TQ_ATTN = 512
TS_CONV = 256
RC_CONV = 64
HALO = 32
TB_RANK = 512
TB_ROWS = 256
TM_EXP = 256

bf16 = jnp.bfloat16
f32 = jnp.float32


def _cparams(n_axes):
    return pltpu.CompilerParams(
        dimension_semantics=("arbitrary",) * n_axes, vmem_limit_bytes=VMEM_LIMIT)


def _inproj_body(x_ref, g1_ref, w_ref, qg_ref, kg_ref, bd_ref,
                 hglu_ref, qt_ref, k_ref, vt_ref):
    x = x_ref[...]
    ms = jnp.mean(x * x, axis=-1, keepdims=True)
    xn = (x * lax.rsqrt(ms + EPS) * g1_ref[...]).astype(bf16)

    def mm(c0, n):
        return jnp.dot(xn, w_ref[:, c0:c0 + n], preferred_element_type=f32)

    nt = 512
    for c in range(0, D_CONV, nt):
        a = mm(c, nt)
        b = mm(D_CONV + c, nt)
        hglu_ref[:, c:c + nt] = a * jax.nn.sigmoid(b)

    bd = bd_ref[...]
    sl = 256
    uq = mm(2 * D_CONV, D_Q)
    uk = mm(2 * D_CONV + D_Q, D_Q)
    uv = mm(2 * D_CONV + 2 * D_Q, D_V)

    def qk_norm(u, gain):
        ms64 = jnp.dot((u * u).astype(bf16), bd, preferred_element_type=f32)
        return u * lax.rsqrt(ms64 + EPS) * gain

    for c in range(0, D_Q, sl):
        qn = qk_norm(uq[:, c:c + sl], qg_ref[:, c:c + sl])
        qt_ref[c:c + sl, :] = qn.T.astype(bf16)
    for c in range(0, D_Q, sl):
        k_ref[:, c:c + sl] = qk_norm(uk[:, c:c + sl], kg_ref[:, c:c + sl]).astype(bf16)
    for c in range(0, D_V, sl):
        vt_ref[0, c:c + sl, :] = uv[:, c:c + sl].T.astype(bf16)


def _inproj(x2, g1, w_bf, qg, kg, bd):
    s = x2.shape[0]
    tm = TM_PROJ
    assert tm == TK_ATTN and s % tm == 0
    const = lambda i: (0, 0)
    return pl.pallas_call(
        _inproj_body,
        grid=(s // tm,),
        in_specs=[
            pl.BlockSpec((tm, D_MODEL), lambda i: (i, 0)),
            pl.BlockSpec((1, D_MODEL), const),
            pl.BlockSpec((D_MODEL, D_IN), const, pipeline_mode=pl.Buffered(1)),
            pl.BlockSpec((1, D_Q), const),
            pl.BlockSpec((1, D_Q), const),
            pl.BlockSpec((256, 256), const),
        ],
        out_specs=[
            pl.BlockSpec((tm, D_CONV), lambda i: (i, 0)),
            pl.BlockSpec((D_Q, tm), lambda i: (0, i)),
            pl.BlockSpec((tm, D_Q), lambda i: (i, 0)),
            pl.BlockSpec((1, D_V, tm), lambda i: (i, 0, 0)),
        ],
        out_shape=[
            jax.ShapeDtypeStruct((s, D_CONV), f32),
            jax.ShapeDtypeStruct((D_Q, s), bf16),
            jax.ShapeDtypeStruct((s, D_Q), bf16),
            jax.ShapeDtypeStruct((s // tm, D_V, tm), bf16),
        ],
        compiler_params=_cparams(1),
        name="inproj",
    )(x2, g1, w_bf, qg, kg, bd)


def _conv_body(prev_ref, cur_ref, w_ref, b_ref, lg_ref, lb_ref, o_ref, pad_ref, acc_ref):
    i = pl.program_id(0)
    ts, rc = TS_CONV, RC_CONV
    pad_ref[0:HALO, :] = jnp.where(i > 0, prev_ref[...], 0.0)
    pad_ref[HALO:HALO + ts, :] = cur_ref[...]
    pad_ref[HALO + ts:, :] = jnp.zeros((SUBLANES, D_CONV), f32)
    off = HALO - (CONV_WIDTH - 1)
    for c0 in range(0, D_CONV, LANES):
        cols = slice(c0, c0 + LANES)
        for r0 in range(0, ts, rc):
            acc = jnp.broadcast_to(b_ref[:, cols], (rc, LANES))
            for a in range(SUBLANES):
                qa = None
                for jp in range(a, off + CONV_WIDTH, SUBLANES):
                    j = jp - off
                    if j < 0:
                        continue
                    base = r0 + jp - a
                    term = w_ref[j:j + 1, cols] * pad_ref[base:base + rc + SUBLANES, cols]
                    qa = term if qa is None else qa + term
                acc = acc + qa[a:a + rc, :]
            acc_ref[r0:r0 + rc, cols] = acc
    acc = acc_ref[...]
    mu = jnp.mean(acc, axis=-1, keepdims=True)
    xc = acc - mu
    var = jnp.mean(xc * xc, axis=-1, keepdims=True)
    y = xc * lax.rsqrt(var + EPS) * lg_ref[...] + lb_ref[...]
    o_ref[...] = (y * jax.nn.sigmoid(y)).astype(bf16)


def _conv(hglu, w, b, lg, lb):
    s = hglu.shape[0]
    ts = TS_CONV
    const = lambda i: (0, 0)
    return pl.pallas_call(
        _conv_body,
        grid=(s // ts,),
        in_specs=[
            pl.BlockSpec((HALO, D_CONV), lambda i: (jnp.maximum(i * (ts // HALO) - 1, 0), 0)),
            pl.BlockSpec((ts, D_CONV), lambda i: (i, 0)),
            pl.BlockSpec((CONV_WIDTH, D_CONV), const),
            pl.BlockSpec((1, D_CONV), const),
            pl.BlockSpec((1, D_CONV), const),
            pl.BlockSpec((1, D_CONV), const),
        ],
        out_specs=pl.BlockSpec((ts, D_CONV), lambda i: (i, 0)),
        out_shape=jax.ShapeDtypeStruct((s, D_CONV), bf16),
        scratch_shapes=[pltpu.VMEM((HALO + ts + SUBLANES, D_CONV), f32), pltpu.VMEM((ts, D_CONV), f32)],
        compiler_params=_cparams(1),
        name="conv",
    )(hglu, hglu, w, b, lg, lb)


def _attn_body(qt_ref, k_ref, vt_ref, lq1_ref, lk1_ref, lq2_ref, lk2_ref, sg_ref,
               o_ref, m_sc, l_sc, acc_sc, sa_sc, sb_sc):
    tq, tk = TQ_ATTN, TK_ATTN
    i = pl.program_id(1)
    qt = qt_ref[...]
    sub = lax.broadcasted_iota(jnp.int32, qt.shape, 0)
    zero = jnp.zeros_like(qt)
    qst = jnp.concatenate([jnp.where(sub < HEAD_DIM, qt, zero),
                           jnp.where(sub >= HEAD_DIM, qt, zero)], axis=1)

    m_sc[...] = jnp.full(m_sc.shape, NEG_BIG, f32)
    l_sc[...] = jnp.zeros(l_sc.shape, f32)
    acc_sc[...] = jnp.zeros(acc_sc.shape, f32)

    def scores(kb, dst):
        kblk = k_ref[pl.ds(pl.multiple_of(kb * tk, tk), tk), :]
        dst[...] = jnp.dot(kblk, qst, preferred_element_type=f32)

    def consume(src, kb, masked):
        s = src[...]
        if masked:
            key = lax.broadcasted_iota(jnp.int32, s.shape, 0)
            qry = lax.broadcasted_iota(jnp.int32, s.shape, 1)
            vis = (key // CHUNK) <= ((qry % tq) // CHUNK)
            s = jnp.where(vis, s, NEG_BIG)
        m_old = m_sc[...]
        m_new = jnp.maximum(m_old, jnp.max(s, axis=0, keepdims=True))
        alpha = jnp.exp2(m_old - m_new)
        p = jnp.exp2(s - m_new)
        l_sc[...] = alpha * l_sc[...] + jnp.sum(p, axis=0, keepdims=True)
        acc_sc[...] = alpha * acc_sc[...] + jnp.dot(vt_ref[kb], p.astype(bf16),
                                                    preferred_element_type=f32)
        m_sc[...] = m_new

    last_full = jnp.maximum(i - 1, 0)
    scores(i, sa_sc)
    scores(0, sb_sc)
    consume(sa_sc, i, True)

    def pair(j, carry):
        scores(2 * j + 1, sa_sc)
        consume(sb_sc, 2 * j, False)
        scores(jnp.minimum(2 * j + 2, last_full), sb_sc)
        consume(sa_sc, 2 * j + 1, False)
        return carry

    lax.fori_loop(0, i // 2, pair, 0)

    @pl.when(i % 2 == 1)
    def _():
        consume(sb_sc, i - 1, False)

    lam = (jnp.exp(jnp.sum(lq1_ref[...] * lk1_ref[...], axis=-1, keepdims=True))
           - jnp.exp(jnp.sum(lq2_ref[...] * lk2_ref[...], axis=-1, keepdims=True))
           + LAMBDA_INIT)
    o_all = acc_sc[...] / l_sc[...]
    o = (o_all[:, :tq] - lam * o_all[:, tq:]).T
    ms = jnp.mean(o * o, axis=-1, keepdims=True)
    y = o * lax.rsqrt(ms + EPS) * sg_ref[...] * (1.0 - LAMBDA_INIT)
    o_ref[...] = y.astype(bf16)


def _attn(qt, k, vt, lq1, lk1, lq2, lk2, sg):
    s = k.shape[0]
    tq, tk = TQ_ATTN, TK_ATTN
    assert tq == tk and s % tq == 0
    vec = lambda h, i: (0, 0)
    return pl.pallas_call(
        _attn_body,
        grid=(N_HEADS, s // tq),
        in_specs=[
            pl.BlockSpec((2 * HEAD_DIM, tq), lambda h, i: (h, i)),
            pl.BlockSpec((s, 2 * HEAD_DIM), lambda h, i: (0, h)),
            pl.BlockSpec((s // tk, V_DIM, tk), lambda h, i: (0, h, 0)),
            pl.BlockSpec((1, HEAD_DIM), vec),
            pl.BlockSpec((1, HEAD_DIM), vec),
            pl.BlockSpec((1, HEAD_DIM), vec),
            pl.BlockSpec((1, HEAD_DIM), vec),
            pl.BlockSpec((1, V_DIM), vec),
        ],
        out_specs=pl.BlockSpec((tq, V_DIM), lambda h, i: (i, h)),
        out_shape=jax.ShapeDtypeStruct((s, D_V), bf16),
        scratch_shapes=[
            pltpu.VMEM((1, 2 * tq), f32),
            pltpu.VMEM((1, 2 * tq), f32),
            pltpu.VMEM((V_DIM, 2 * tq), f32),
            pltpu.VMEM((tk, 2 * tq), f32),
            pltpu.VMEM((tk, 2 * tq), f32),
        ],
        compiler_params=_cparams(2),
        name="attn",
    )(qt, k, vt, lq1, lk1, lq2, lk2, sg)


def _outproj_body(yc_ref, ya_ref, x_ref, w_ref, g2_ref, wr_ref, br_ref, h_ref, r_ref):
    half = TM_PROJ // 2
    for r0 in range(0, TM_PROJ, half):
        rows = slice(r0, r0 + half)
        h = (jnp.dot(yc_ref[rows, :], w_ref[0:D_CONV, :], preferred_element_type=f32)
             + jnp.dot(ya_ref[rows, :], w_ref[D_CONV:, :], preferred_element_type=f32)
             + x_ref[rows, :])
        h_ref[rows, :] = h
        ms = jnp.mean(h * h, axis=-1, keepdims=True)
        t = h * lax.rsqrt(ms + EPS) * g2_ref[...]
        t_hi = t.astype(bf16)
        t_lo = (t - t_hi.astype(f32)).astype(bf16)
        hh = jnp.dot(t_hi, wr_ref[...], preferred_element_type=f32)
        lh = jnp.dot(t_lo, wr_ref[:, 0:LANES], preferred_element_type=f32)
        logits = hh[:, 0:LANES] + hh[:, LANES:] + lh + br_ref[...]
        r_ref[rows, :] = _route(logits)


def _route(logits):
    lane = lax.broadcasted_iota(jnp.int32, logits.shape, 1)
    lanef = lane.astype(f32)
    big = float(LANES)
    is_g = lane < N_GROUPS
    gl = jnp.where(is_g, logits, NEG_BIG)
    gmax = jnp.max(gl, axis=1, keepdims=True)
    gidx = jnp.min(jnp.where(gl == gmax, lanef, big), axis=1, keepdims=True)
    gsum = jnp.sum(jnp.where(is_g, jnp.exp(gl - gmax), 0.0), axis=1, keepdims=True)
    gw = 1.0 / gsum

    e_of_lane = lane - N_GROUPS
    in_group = ((e_of_lane >= 0) & (e_of_lane < N_EXPERTS)
                & ((e_of_lane // EXPERTS_PER_GROUP).astype(f32) == gidx))
    el = jnp.where(in_group, logits, NEG_BIG)
    v1 = jnp.max(el, axis=1, keepdims=True)
    i1 = jnp.min(jnp.where(el == v1, lanef, big), axis=1, keepdims=True)
    el2 = jnp.where(lanef == i1, NEG_BIG, el)
    v2 = jnp.max(el2, axis=1, keepdims=True)
    i2 = jnp.min(jnp.where(el2 == v2, lanef, big), axis=1, keepdims=True)
    e21 = jnp.exp(v2 - v1)
    den = 1.0 + e21
    w1 = gw * (1.0 / den)
    w2 = gw * (e21 / den)
    return jnp.where(lane == 0, i1 - N_GROUPS,
                     jnp.where(lane == 1, i2 - N_GROUPS,
                               jnp.where(lane == 2, w1, jnp.where(lane == 3, w2, 0.0))))


def _outproj(yc, ya, x2, w_bf, g2, wr, br):
    s = x2.shape[0]
    tm = TM_PROJ
    const = lambda i: (0, 0)
    row = lambda i: (i, 0)
    return pl.pallas_call(
        _outproj_body,
        grid=(s // tm,),
        in_specs=[
            pl.BlockSpec((tm, D_CONV), row),
            pl.BlockSpec((tm, D_V), row),
            pl.BlockSpec((tm, D_MODEL), row),
            pl.BlockSpec((D_CONV + D_V, D_MODEL), const, pipeline_mode=pl.Buffered(1)),
            pl.BlockSpec((1, D_MODEL), const),
            pl.BlockSpec((D_MODEL, 2 * LANES), const),
            pl.BlockSpec((1, LANES), const),
        ],
        out_specs=[pl.BlockSpec((tm, D_MODEL), row), pl.BlockSpec((tm, LANES), row)],
        out_shape=[jax.ShapeDtypeStruct((s, D_MODEL), f32), jax.ShapeDtypeStruct((s, LANES), f32)],
        compiler_params=_cparams(1),
        name="outproj",
    )(yc, ya, x2, w_bf, g2, wr, br)


def _rank_body(r_ref, rank_ref, cnt_ref, run_sc):
    tb = TB_RANK
    i = pl.program_id(0)

    @pl.when(i == 0)
    def _():
        run_sc[...] = jnp.zeros(run_sc.shape, f32)

    r = r_ref[...]
    lane = lax.broadcasted_iota(jnp.int32, r.shape, 1)
    lanef = lane.astype(f32)
    oh1 = (lanef == r[:, 0:1]).astype(f32)
    oh2 = (lanef == r[:, 1:2]).astype(f32)
    both = oh1 + oh2
    row = lax.broadcasted_iota(jnp.int32, (tb, tb), 0)
    col = lax.broadcasted_iota(jnp.int32, (tb, tb), 1)
    tri = (row > col).astype(bf16)
    before = jnp.dot(tri, both.astype(bf16), preferred_element_type=f32) + run_sc[...]
    rank1 = jnp.sum(before * oh1, axis=1, keepdims=True)
    rank2 = jnp.sum(before * oh2, axis=1, keepdims=True)
    rank_ref[...] = jnp.where(lane == 0, rank1, jnp.where(lane == 1, rank2, 0.0))
    run_sc[...] = run_sc[...] + jnp.sum(both, axis=0, keepdims=True)
    cnt_ref[...] = run_sc[...]


def _rank(r):
    s = r.shape[0]
    tb = TB_RANK
    return pl.pallas_call(
        _rank_body,
        grid=(s // tb,),
        in_specs=[pl.BlockSpec((tb, LANES), lambda i: (i, 0))],
        out_specs=[pl.BlockSpec((tb, LANES), lambda i: (i, 0)),
                   pl.BlockSpec((1, LANES), lambda i: (0, 0))],
        out_shape=[jax.ShapeDtypeStruct((s, LANES), f32), jax.ShapeDtypeStruct((1, LANES), f32)],
        scratch_shapes=[pltpu.VMEM((1, LANES), f32)],
        compiler_params=_cparams(1),
        name="rank",
    )(r)


def _row_copy(src, dst, sem):
    return pltpu.make_async_copy(src, dst, sem)


def _dispatch_body(pos1_ref, pos2_ref, h_ref, g2_ref, xs_init_ref, xs_ref, tbuf, sems):
    del xs_init_ref
    tb = TB_ROWS
    i = pl.program_id(0)
    n = pl.num_programs(0)
    slot = i % 2

    def wait_slot(sl):
        for _ in range(2):
            _row_copy(tbuf.at[sl], xs_ref.at[pl.ds(0, tb)], sems.at[sl]).wait()

    @pl.when(i >= 2)
    def _():
        wait_slot(slot)

    h = h_ref[...]
    ms = jnp.mean(h * h, axis=-1, keepdims=True)
    tbuf[slot] = h * lax.rsqrt(ms + EPS) * g2_ref[...]

    def issue(r, carry):
        src = tbuf.at[slot, pl.ds(r, 1)]
        _row_copy(src, xs_ref.at[pl.ds(pos1_ref[r], 1)], sems.at[slot]).start()
        _row_copy(src, xs_ref.at[pl.ds(pos2_ref[r], 1)], sems.at[slot]).start()
        return carry

    lax.fori_loop(0, tb, issue, 0, unroll=8)

    @pl.when(i == n - 1)
    def _():
        wait_slot(slot)

        @pl.when(n >= 2)
        def _():
            wait_slot(1 - slot)


def _dispatch(pos1, pos2, h, g2, n_rows):
    s = h.shape[0]
    tb = TB_ROWS
    return pl.pallas_call(
        _dispatch_body,
        grid=(s // tb,),
        in_specs=[
            pl.BlockSpec((tb,), lambda i: (i,), memory_space=pltpu.SMEM),
            pl.BlockSpec((tb,), lambda i: (i,), memory_space=pltpu.SMEM),
            pl.BlockSpec((tb, D_MODEL), lambda i: (i, 0)),
            pl.BlockSpec((1, D_MODEL), lambda i: (0, 0)),
            pl.BlockSpec(memory_space=pl.ANY),
        ],
        out_specs=pl.BlockSpec(memory_space=pl.ANY),
        out_shape=jax.ShapeDtypeStruct((n_rows, D_MODEL), f32),
        input_output_aliases={4: 0},
        scratch_shapes=[pltpu.VMEM((2, tb, D_MODEL), f32), pltpu.SemaphoreType.DMA((2,))],
        compiler_params=_cparams(1),
        name="dispatch",
    )(pos1, pos2, h, g2, jnp.zeros((n_rows, D_MODEL), f32))


def _experts_body(te_ref, nv_ref, xs_ref, wg_ref, wu_ref, wd_ref, y_ref, wg_sc, wu_sc, wd_sc):
    i = pl.program_id(0)

    @pl.when(i < nv_ref[0])
    def _():
        prev = te_ref[jnp.maximum(i - 1, 0)]

        @pl.when((i == 0) | (te_ref[i] != prev))
        def _():
            wg_sc[...] = wg_ref[0].astype(bf16)
            wu_sc[...] = wu_ref[0].astype(bf16)
            wd_sc[...] = wd_ref[0].astype(bf16)

        x = xs_ref[...].astype(bf16)
        g = jnp.dot(x, wg_sc[...], preferred_element_type=f32)
        u = jnp.dot(x, wu_sc[...], preferred_element_type=f32)
        hm = (g * jax.nn.sigmoid(g) * u).astype(bf16)
        y_ref[...] = jnp.dot(hm, wd_sc[...], preferred_element_type=f32)

    @pl.when(i >= nv_ref[0])
    def _():
        y_ref[...] = jnp.zeros(y_ref.shape, f32)


def _experts(tile_expert, n_valid, xs, w_gate, w_up, w_down):
    n_rows = xs.shape[0]
    tm = TM_EXP
    nt = n_rows // tm
    rows = lambda i, te, nv: (jnp.minimum(i, nv[0] - 1), 0)
    wsel = lambda i, te, nv: (te[i], 0, 0)
    grid_spec = pltpu.PrefetchScalarGridSpec(
        num_scalar_prefetch=2,
        grid=(nt,),
        in_specs=[
            pl.BlockSpec((tm, D_MODEL), rows),
            pl.BlockSpec((1, D_MODEL, D_EXPERT), wsel),
            pl.BlockSpec((1, D_MODEL, D_EXPERT), wsel),
            pl.BlockSpec((1, D_EXPERT, D_MODEL), wsel),
        ],
        out_specs=pl.BlockSpec((tm, D_MODEL), lambda i, te, nv: (i, 0)),
        scratch_shapes=[
            pltpu.VMEM((D_MODEL, D_EXPERT), bf16),
            pltpu.VMEM((D_MODEL, D_EXPERT), bf16),
            pltpu.VMEM((D_EXPERT, D_MODEL), bf16),
        ],
    )
    return pl.pallas_call(
        _experts_body,
        grid_spec=grid_spec,
        out_shape=jax.ShapeDtypeStruct((n_rows, D_MODEL), f32),
        compiler_params=_cparams(1),
        name="experts",
    )(tile_expert, n_valid, xs, w_gate, w_up, w_down)


def _combine_body(p1_ref, p2_ref, p1n_ref, p2n_ref, h_ref, r_ref, y_ref, o_ref, ybuf, sems):
    tb = TB_ROWS
    i = pl.program_id(0)
    n = pl.num_programs(0)
    slot = i % 2

    def issue_block(pa_ref, pb_ref, sl):
        def issue(r, carry):
            _row_copy(y_ref.at[pl.ds(pa_ref[r], 1)], ybuf.at[sl, 0, pl.ds(r, 1)], sems.at[sl]).start()
            _row_copy(y_ref.at[pl.ds(pb_ref[r], 1)], ybuf.at[sl, 1, pl.ds(r, 1)], sems.at[sl]).start()
            return carry
        lax.fori_loop(0, tb, issue, 0, unroll=8)

    @pl.when(i == 0)
    def _():
        issue_block(p1_ref, p2_ref, 0)

    @pl.when(i + 1 < n)
    def _():
        issue_block(p1n_ref, p2n_ref, 1 - slot)

    for k in range(2):
        _row_copy(y_ref.at[pl.ds(0, tb)], ybuf.at[slot, k], sems.at[slot]).wait()

    r = r_ref[...]
    o_ref[...] = h_ref[...] + r[:, 2:3] * ybuf[slot, 0] + r[:, 3:4] * ybuf[slot, 1]


def _combine(pos1, pos2, h, r, y):
    s = h.shape[0]
    tb = TB_ROWS
    nb = s // tb
    cur = lambda i: (i,)
    nxt = lambda i: (jnp.minimum(i + 1, nb - 1),)
    smem = functools.partial(pl.BlockSpec, (tb,), memory_space=pltpu.SMEM)
    return pl.pallas_call(
        _combine_body,
        grid=(nb,),
        in_specs=[
            smem(cur), smem(cur), smem(nxt), smem(nxt),
            pl.BlockSpec((tb, D_MODEL), lambda i: (i, 0)),
            pl.BlockSpec((tb, LANES), lambda i: (i, 0)),
            pl.BlockSpec(memory_space=pl.ANY),
        ],
        out_specs=pl.BlockSpec((tb, D_MODEL), lambda i: (i, 0)),
        out_shape=jax.ShapeDtypeStruct((s, D_MODEL), f32),
        scratch_shapes=[pltpu.VMEM((2, 2, tb, D_MODEL), f32), pltpu.SemaphoreType.DMA((2,))],
        compiler_params=_cparams(1),
        name="combine",
    )(pos1, pos2, pos1, pos2, h, r, y)


def kernel(x, norm1_g, w_in, conv_dw_kernel, conv_dw_bias, conv_ln_g, conv_ln_b, q_norm_g, k_norm_g,
           lambda_q1, lambda_k1, lambda_q2, lambda_k2, subln_g, w_out, norm2_g, w_group, b_group,
           w_router, b_router, w_gate, w_up, w_down):
    b, s, d = x.shape
    assert b == 1 and d == D_MODEL and norm1_g.shape[0] == 1
    x2 = x.reshape(s, d)

    w_in_bf = w_in[0].astype(bf16)
    w_out_bf = w_out[0].astype(bf16)
    qg = jnp.tile(q_norm_g[0].reshape(1, 2 * HEAD_DIM), (1, N_HEADS)) * (HEAD_DIM ** -0.5 * math.log2(math.e))
    kg = jnp.tile(k_norm_g[0].reshape(1, 2 * HEAD_DIM), (1, N_HEADS))
    blk = jnp.arange(256) // HEAD_DIM
    bd = jnp.where(blk[:, None] == blk[None, :], 1.0 / HEAD_DIM, 0.0).astype(bf16)
    n_r = N_GROUPS + N_EXPERTS
    wr = jnp.concatenate(
        [w_group[0], jnp.transpose(w_router[0], (1, 0, 2)).reshape(d, N_EXPERTS),
         jnp.zeros((d, LANES - n_r), f32)], axis=1)
    wr_hi = wr.astype(bf16)
    wr = jnp.concatenate([wr_hi, (wr - wr_hi.astype(f32)).astype(bf16)], axis=1)
    br = jnp.concatenate([b_group[0], b_router[0].reshape(N_EXPERTS), jnp.zeros((LANES - n_r,), f32)])[None]
    row = lambda a: a.reshape(1, -1)

    hglu, qt, k, vt = _inproj(x2, norm1_g, w_in_bf, qg, kg, bd)
    y_conv = _conv(hglu, conv_dw_kernel[0], row(conv_dw_bias[0]), row(conv_ln_g[0]), row(conv_ln_b[0]))
    y_attn = _attn(qt, k, vt, row(lambda_q1[0]), row(lambda_k1[0]), row(lambda_q2[0]), row(lambda_k2[0]),
                   row(subln_g[0]))
    h, r = _outproj(y_conv, y_attn, x2, w_out_bf, norm2_g, wr, br)

    rank, cnt = _rank(r)
    tm = TM_EXP
    counts = cnt[0, :N_EXPERTS].astype(jnp.int32)
    padded = ((counts + tm - 1) // tm) * tm
    ends = jnp.cumsum(padded)
    offsets = ends - padded
    e1 = r[:, 0].astype(jnp.int32)
    e2 = r[:, 1].astype(jnp.int32)
    pos1 = offsets[e1] + rank[:, 0].astype(jnp.int32)
    pos2 = offsets[e2] + rank[:, 1].astype(jnp.int32)
    n_rows = TOP_K * s + N_EXPERTS * tm
    nt = n_rows // tm
    n_valid = (ends[-1] // tm).astype(jnp.int32)
    tile_start = jnp.minimum(jnp.arange(nt, dtype=jnp.int32), n_valid - 1) * tm
    tile_expert = jnp.sum(tile_start[:, None] >= ends[None, :], axis=1).astype(jnp.int32)

    xs = _dispatch(pos1, pos2, h, norm2_g, n_rows)
    ys = _experts(tile_expert, n_valid.reshape(1), xs, w_gate[0], w_up[0], w_down[0])
    out = _combine(pos1, pos2, h, r, ys)
    return out.reshape(b, s, d)
```

```python
import functools
import math

import jax
import jax.numpy as jnp
from jax import lax
from jax.experimental import pallas as pl
from jax.experimental.pallas import tpu as pltpu

D_MODEL = 2048
D_CONV = 1024
N_HEADS = 8
HEAD_DIM = 64
V_DIM = 128
CONV_WIDTH = 31
CHUNK = 64
N_GROUPS = 4
EXPERTS_PER_GROUP = 8
N_EXPERTS = 32
TOP_K = 2
D_EXPERT = 512
D_Q = 1024
D_V = 1024
D_IN = 2 * D_CONV + 2 * D_Q + D_V
EPS = 1e-6
LAMBDA_INIT = 0.8 - 0.6 * math.exp(-0.3 * 0)

LANES = 128
SUBLANES = 8
VMEM_LIMIT = 56 * 1024 * 1024
NEG_BIG = -1e30

TM_PROJ = 512
TK_ATTN = 512
TQ_ATTN = 512
TS_CONV = 256
RC_CONV = 64
HALO = 32
TB_RANK = 512
TB_ROWS = 256
TM_EXP = 256

bf16 = jnp.bfloat16
f32 = jnp.float32


def _cparams(n_axes):
    return pltpu.CompilerParams(
        dimension_semantics=("arbitrary",) * n_axes, vmem_limit_bytes=VMEM_LIMIT)


def _inproj_body(x_ref, g1_ref, w_ref, qg_ref, kg_ref, bd_ref,
                 hglu_ref, qt_ref, k_ref, vt_ref):
    x = x_ref[...]
    ms = jnp.mean(x * x, axis=-1, keepdims=True)
    xn = (x * lax.rsqrt(ms + EPS) * g1_ref[...]).astype(bf16)

    def mm(c0, n):
        return jnp.dot(xn, w_ref[:, c0:c0 + n], preferred_element_type=f32)

    nt = 512
    for c in range(0, D_CONV, nt):
        a = mm(c, nt)
        b = mm(D_CONV + c, nt)
        hglu_ref[:, c:c + nt] = a * jax.nn.sigmoid(b)

    bd = bd_ref[...]
    sl = 256
    uq = mm(2 * D_CONV, D_Q)
    uk = mm(2 * D_CONV + D_Q, D_Q)
    uv = mm(2 * D_CONV + 2 * D_Q, D_V)

    def qk_norm(u, gain):
        ms64 = jnp.dot((u * u).astype(bf16), bd, preferred_element_type=f32)
        return u * lax.rsqrt(ms64 + EPS) * gain

    for c in range(0, D_Q, sl):
        qn = qk_norm(uq[:, c:c + sl], qg_ref[:, c:c + sl])
        qt_ref[c:c + sl, :] = qn.T.astype(bf16)
    for c in range(0, D_Q, sl):
        k_ref[:, c:c + sl] = qk_norm(uk[:, c:c + sl], kg_ref[:, c:c + sl]).astype(bf16)
    for c in range(0, D_V, sl):
        vt_ref[0, c:c + sl, :] = uv[:, c:c + sl].T.astype(bf16)


def _inproj(x2, g1, w_bf, qg, kg, bd):
    s = x2.shape[0]
    tm = TM_PROJ
    assert tm == TK_ATTN and s % tm == 0
    const = lambda i: (0, 0)
    return pl.pallas_call(
        _inproj_body,
        grid=(s // tm,),
        in_specs=[
            pl.BlockSpec((tm, D_MODEL), lambda i: (i, 0)),
            pl.BlockSpec((1, D_MODEL), const),
            pl.BlockSpec((D_MODEL, D_IN), const, pipeline_mode=pl.Buffered(1)),
            pl.BlockSpec((1, D_Q), const),
            pl.BlockSpec((1, D_Q), const),
            pl.BlockSpec((256, 256), const),
        ],
        out_specs=[
            pl.BlockSpec((tm, D_CONV), lambda i: (i, 0)),
            pl.BlockSpec((D_Q, tm), lambda i: (0, i)),
            pl.BlockSpec((tm, D_Q), lambda i: (i, 0)),
            pl.BlockSpec((1, D_V, tm), lambda i: (i, 0, 0)),
        ],
        out_shape=[
            jax.ShapeDtypeStruct((s, D_CONV), f32),
            jax.ShapeDtypeStruct((D_Q, s), bf16),
            jax.ShapeDtypeStruct((s, D_Q), bf16),
            jax.ShapeDtypeStruct((s // tm, D_V, tm), bf16),
        ],
        compiler_params=_cparams(1),
        name="inproj",
    )(x2, g1, w_bf, qg, kg, bd)


def _conv_body(prev_ref, cur_ref, w_ref, b_ref, lg_ref, lb_ref, o_ref, pad_ref, acc_ref):
    i = pl.program_id(0)
    ts, rc = TS_CONV, RC_CONV
    pad_ref[0:HALO, :] = jnp.where(i > 0, prev_ref[...], 0.0)
    pad_ref[HALO:HALO + ts, :] = cur_ref[...]
    pad_ref[HALO + ts:, :] = jnp.zeros((SUBLANES, D_CONV), f32)
    off = HALO - (CONV_WIDTH - 1)
    for c0 in range(0, D_CONV, LANES):
        cols = slice(c0, c0 + LANES)
        for r0 in range(0, ts, rc):
            acc = jnp.broadcast_to(b_ref[:, cols], (rc, LANES))
            for a in range(SUBLANES):
                qa = None
                for jp in range(a, off + CONV_WIDTH, SUBLANES):
                    j = jp - off
                    if j < 0:
                        continue
                    base = r0 + jp - a
                    term = w_ref[j:j + 1, cols] * pad_ref[base:base + rc + SUBLANES, cols]
                    qa = term if qa is None else qa + term
                acc = acc + qa[a:a + rc, :]
            acc_ref[r0:r0 + rc, cols] = acc
    acc = acc_ref[...]
    mu = jnp.mean(acc, axis=-1, keepdims=True)
    xc = acc - mu
    var = jnp.mean(xc * xc, axis=-1, keepdims=True)
    y = xc * lax.rsqrt(var + EPS) * lg_ref[...] + lb_ref[...]
    o_ref[...] = (y * jax.nn.sigmoid(y)).astype(bf16)


def _conv(hglu, w, b, lg, lb):
    s = hglu.shape[0]
    ts = TS_CONV
    const = lambda i: (0, 0)
    return pl.pallas_call(
        _conv_body,
        grid=(s // ts,),
        in_specs=[
            pl.BlockSpec((HALO, D_CONV), lambda i: (jnp.maximum(i * (ts // HALO) - 1, 0), 0)),
            pl.BlockSpec((ts, D_CONV), lambda i: (i, 0)),
            pl.BlockSpec((CONV_WIDTH, D_CONV), const),
            pl.BlockSpec((1, D_CONV), const),
            pl.BlockSpec((1, D_CONV), const),
            pl.BlockSpec((1, D_CONV), const),
        ],
        out_specs=pl.BlockSpec((ts, D_CONV), lambda i: (i, 0)),
        out_shape=jax.ShapeDtypeStruct((s, D_CONV), bf16),
        scratch_shapes=[pltpu.VMEM((HALO + ts + SUBLANES, D_CONV), f32), pltpu.VMEM((ts, D_CONV), f32)],
        compiler_params=_cparams(1),
        name="conv",
    )(hglu, hglu, w, b, lg, lb)


def _attn_body(qt_ref, k_ref, vt_ref, lq1_ref, lk1_ref, lq2_ref, lk2_ref, sg_ref,
               o_ref, m_sc, l_sc, acc_sc, sa_sc, sb_sc):
    tq, tk = TQ_ATTN, TK_ATTN
    i = pl.program_id(1)
    qt = qt_ref[...]
    sub = lax.broadcasted_iota(jnp.int32, qt.shape, 0)
    zero = jnp.zeros_like(qt)
    qst = jnp.concatenate([jnp.where(sub < HEAD_DIM, qt, zero),
                           jnp.where(sub >= HEAD_DIM, qt, zero)], axis=1)

    m_sc[...] = jnp.full(m_sc.shape, NEG_BIG, f32)
    l_sc[...] = jnp.zeros(l_sc.shape, f32)
    acc_sc[...] = jnp.zeros(acc_sc.shape, f32)

    def scores(kb, dst):
        kblk = k_ref[pl.ds(pl.multiple_of(kb * tk, tk), tk), :]
        dst[...] = jnp.dot(kblk, qst, preferred_element_type=f32)

    def consume(src, kb, masked):
        s = src[...]
        if masked:
            key = lax.broadcasted_iota(jnp.int32, s.shape, 0)
            qry = lax.broadcasted_iota(jnp.int32, s.shape, 1)
            vis = (key // CHUNK) <= ((qry % tq) // CHUNK)
            s = jnp.where(vis, s, NEG_BIG)
        m_old = m_sc[...]
        m_new = jnp.maximum(m_old, jnp.max(s, axis=0, keepdims=True))
        alpha = jnp.exp2(m_old - m_new)
        p = jnp.exp2(s - m_new)
        l_sc[...] = alpha * l_sc[...] + jnp.sum(p, axis=0, keepdims=True)
        acc_sc[...] = alpha * acc_sc[...] + jnp.dot(vt_ref[kb], p.astype(bf16),
                                                    preferred_element_type=f32)
        m_sc[...] = m_new

    last_full = jnp.maximum(i - 1, 0)
    scores(i, sa_sc)
    scores(0, sb_sc)
    consume(sa_sc, i, True)

    def pair(j, carry):
        scores(2 * j + 1, sa_sc)
        consume(sb_sc, 2 * j, False)
        scores(jnp.minimum(2 * j + 2, last_full), sb_sc)
        consume(sa_sc, 2 * j + 1, False)
        return carry

    lax.fori_loop(0, i // 2, pair, 0)

    @pl.when(i % 2 == 1)
    def _():
        consume(sb_sc, i - 1, False)

    lam = (jnp.exp(jnp.sum(lq1_ref[...] * lk1_ref[...], axis=-1, keepdims=True))
           - jnp.exp(jnp.sum(lq2_ref[...] * lk2_ref[...], axis=-1, keepdims=True))
           + LAMBDA_INIT)
    o_all = acc_sc[...] / l_sc[...]
    o = (o_all[:, :tq] - lam * o_all[:, tq:]).T
    ms = jnp.mean(o * o, axis=-1, keepdims=True)
    y = o * lax.rsqrt(ms + EPS) * sg_ref[...] * (1.0 - LAMBDA_INIT)
    o_ref[...] = y.astype(bf16)


def _attn(qt, k, vt, lq1, lk1, lq2, lk2, sg):
    s = k.shape[0]
    tq, tk = TQ_ATTN, TK_ATTN
    assert tq == tk and s % tq == 0
    vec = lambda h, i: (0, 0)
    return pl.pallas_call(
        _attn_body,
        grid=(N_HEADS, s // tq),
        in_specs=[
            pl.BlockSpec((2 * HEAD_DIM, tq), lambda h, i: (h, i)),
            pl.BlockSpec((s, 2 * HEAD_DIM), lambda h, i: (0, h)),
            pl.BlockSpec((s // tk, V_DIM, tk), lambda h, i: (0, h, 0)),
            pl.BlockSpec((1, HEAD_DIM), vec),
            pl.BlockSpec((1, HEAD_DIM), vec),
            pl.BlockSpec((1, HEAD_DIM), vec),
            pl.BlockSpec((1, HEAD_DIM), vec),
            pl.BlockSpec((1, V_DIM), vec),
        ],
        out_specs=pl.BlockSpec((tq, V_DIM), lambda h, i: (i, h)),
        out_shape=jax.ShapeDtypeStruct((s, D_V), bf16),
        scratch_shapes=[
            pltpu.VMEM((1, 2 * tq), f32),
            pltpu.VMEM((1, 2 * tq), f32),
            pltpu.VMEM((V_DIM, 2 * tq), f32),
            pltpu.VMEM((tk, 2 * tq), f32),
            pltpu.VMEM((tk, 2 * tq), f32),
        ],
        compiler_params=_cparams(2),
        name="attn",
    )(qt, k, vt, lq1, lk1, lq2, lk2, sg)


def _outproj_body(yc_ref, ya_ref, x_ref, w_ref, g2_ref, wr_ref, br_ref, h_ref, r_ref):
    half = TM_PROJ // 2
    for r0 in range(0, TM_PROJ, half):
        rows = slice(r0, r0 + half)
        h = (jnp.dot(yc_ref[rows, :], w_ref[0:D_CONV, :], preferred_element_type=f32)
             + jnp.dot(ya_ref[rows, :], w_ref[D_CONV:, :], preferred_element_type=f32)
             + x_ref[rows, :])
        h_ref[rows, :] = h
        ms = jnp.mean(h * h, axis=-1, keepdims=True)
        t = h * lax.rsqrt(ms + EPS) * g2_ref[...]
        t_hi = t.astype(bf16)
        t_lo = (t - t_hi.astype(f32)).astype(bf16)
        hh = jnp.dot(t_hi, wr_ref[...], preferred_element_type=f32)
        lh = jnp.dot(t_lo, wr_ref[:, 0:LANES], preferred_element_type=f32)
        logits = hh[:, 0:LANES] + hh[:, LANES:] + lh + br_ref[...]
        r_ref[rows, :] = _route(logits)


def _route(logits):
    lane = lax.broadcasted_iota(jnp.int32, logits.shape, 1)
    lanef = lane.astype(f32)
    big = float(LANES)
    is_g = lane < N_GROUPS
    gl = jnp.where(is_g, logits, NEG_BIG)
    gmax = jnp.max(gl, axis=1, keepdims=True)
    gidx = jnp.min(jnp.where(gl == gmax, lanef, big), axis=1, keepdims=True)
    gsum = jnp.sum(jnp.where(is_g, jnp.exp(gl - gmax), 0.0), axis=1, keepdims=True)
    gw = 1.0 / gsum

    e_of_lane = lane - N_GROUPS
    in_group = ((e_of_lane >= 0) & (e_of_lane < N_EXPERTS)
                & ((e_of_lane // EXPERTS_PER_GROUP).astype(f32) == gidx))
    el = jnp.where(in_group, logits, NEG_BIG)
    v1 = jnp.max(el, axis=1, keepdims=True)
    i1 = jnp.min(jnp.where(el == v1, lanef, big), axis=1, keepdims=True)
    el2 = jnp.where(lanef == i1, NEG_BIG, el)
    v2 = jnp.max(el2, axis=1, keepdims=True)
    i2 = jnp.min(jnp.where(el2 == v2, lanef, big), axis=1, keepdims=True)
    e21 = jnp.exp(v2 - v1)
    den = 1.0 + e21
    w1 = gw * (1.0 / den)
    w2 = gw * (e21 / den)
    return jnp.where(lane == 0, i1 - N_GROUPS,
                     jnp.where(lane == 1, i2 - N_GROUPS,
                               jnp.where(lane == 2, w1, jnp.where(lane == 3, w2, 0.0))))


def _outproj(yc, ya, x2, w_bf, g2, wr, br):
    s = x2.shape[0]
    tm = TM_PROJ
    const = lambda i: (0, 0)
    row = lambda i: (i, 0)
    return pl.pallas_call(
        _outproj_body,
        grid=(s // tm,),
        in_specs=[
            pl.BlockSpec((tm, D_CONV), row),
            pl.BlockSpec((tm, D_V), row),
            pl.BlockSpec((tm, D_MODEL), row),
            pl.BlockSpec((D_CONV + D_V, D_MODEL), const, pipeline_mode=pl.Buffered(1)),
            pl.BlockSpec((1, D_MODEL), const),
            pl.BlockSpec((D_MODEL, 2 * LANES), const),
            pl.BlockSpec((1, LANES), const),
        ],
        out_specs=[pl.BlockSpec((tm, D_MODEL), row), pl.BlockSpec((tm, LANES), row)],
        out_shape=[jax.ShapeDtypeStruct((s, D_MODEL), f32), jax.ShapeDtypeStruct((s, LANES), f32)],
        compiler_params=_cparams(1),
        name="outproj",
    )(yc, ya, x2, w_bf, g2, wr, br)


def _rank_body(r_ref, pos_ref, cnt_ref, run_sc, rank_sc):
    tb = TB_RANK
    ph = pl.program_id(0)
    i = pl.program_id(1)
    r = r_ref[...]
    lane = lax.broadcasted_iota(jnp.int32, r.shape, 1)
    lanef = lane.astype(f32)
    oh1 = (lanef == r[:, 0:1]).astype(f32)
    oh2 = (lanef == r[:, 1:2]).astype(f32)

    @pl.when((ph == 0) & (i == 0))
    def _():
        run_sc[...] = jnp.zeros(run_sc.shape, f32)

    @pl.when(ph == 0)
    def _():
        both = oh1 + oh2
        row = lax.broadcasted_iota(jnp.int32, (tb, tb), 0)
        col = lax.broadcasted_iota(jnp.int32, (tb, tb), 1)
        tri = (row > col).astype(bf16)
        before = jnp.dot(tri, both.astype(bf16), preferred_element_type=f32) + run_sc[...]
        rank1 = jnp.sum(before * oh1, axis=1, keepdims=True)
        rank2 = jnp.sum(before * oh2, axis=1, keepdims=True)
        rank_sc[i] = jnp.where(lane == 0, rank1, jnp.where(lane == 1, rank2, 0.0))
        run_sc[...] = run_sc[...] + jnp.sum(both, axis=0, keepdims=True)

    @pl.when(ph == 1)
    def _():
        cnt = run_sc[...]
        tiles = jnp.floor((cnt + (TM_EXP - 1)) * (1.0 / TM_EXP))
        lr = lax.broadcasted_iota(jnp.int32, (LANES, LANES), 0)
        lc = lax.broadcasted_iota(jnp.int32, (LANES, LANES), 1)
        upper = (lr < lc).astype(bf16)
        tiles8 = jnp.broadcast_to(tiles, (SUBLANES, LANES)).astype(bf16)
        offs = jnp.dot(tiles8, upper, preferred_element_type=f32)[0:1, :] * TM_EXP
        rk = rank_sc[i]
        pos1 = jnp.sum(oh1 * offs, axis=1, keepdims=True) + rk[:, 0:1]
        pos2 = jnp.sum(oh2 * offs, axis=1, keepdims=True) + rk[:, 1:2]
        pp = jnp.where(lane == 0, pos1, jnp.where(lane == 1, pos2, 0.0))
        pos_ref[...] = pp.T[0:SUBLANES, :].astype(jnp.int32)
        cnt_ref[...] = cnt


def _rank(r):
    s = r.shape[0]
    tb = TB_RANK
    nb = s // tb
    return pl.pallas_call(
        _rank_body,
        grid=(2, nb),
        in_specs=[pl.BlockSpec((tb, LANES), lambda ph, i: (i, 0))],
        out_specs=[pl.BlockSpec((SUBLANES, tb), lambda ph, i: (0, i * ph)),
                   pl.BlockSpec((1, LANES), lambda ph, i: (0, 0))],
        out_shape=[jax.ShapeDtypeStruct((SUBLANES, s), jnp.int32), jax.ShapeDtypeStruct((1, LANES), f32)],
        scratch_shapes=[pltpu.VMEM((1, LANES), f32), pltpu.VMEM((nb, tb, LANES), f32)],
        compiler_params=_cparams(2),
        name="rank",
    )(r)


def _row_copy(src, dst, sem):
    return pltpu.make_async_copy(src, dst, sem)


def _dispatch_body(pos1_ref, pos2_ref, h_ref, g2_ref, xs_init_ref, xs_ref, tbuf, sems):
    del xs_init_ref
    tb = TB_ROWS
    i = pl.program_id(0)
    n = pl.num_programs(0)
    slot = i % 2

    def wait_slot(sl):
        for _ in range(2):
            _row_copy(tbuf.at[sl], xs_ref.at[pl.ds(0, tb)], sems.at[sl]).wait()

    @pl.when(i >= 2)
    def _():
        wait_slot(slot)

    h = h_ref[...]
    ms = jnp.mean(h * h, axis=-1, keepdims=True)
    tbuf[slot] = h * lax.rsqrt(ms + EPS) * g2_ref[...]

    for r in range(tb):
        src = tbuf.at[slot, pl.ds(r, 1)]
        _row_copy(src, xs_ref.at[pl.ds(pos1_ref[r], 1)], sems.at[slot]).start()
        _row_copy(src, xs_ref.at[pl.ds(pos2_ref[r], 1)], sems.at[slot]).start()

    @pl.when(i == n - 1)
    def _():
        wait_slot(slot)

        @pl.when(n >= 2)
        def _():
            wait_slot(1 - slot)


def _dispatch(pos1, pos2, h, g2, n_rows):
    s = h.shape[0]
    tb = TB_ROWS
    return pl.pallas_call(
        _dispatch_body,
        grid=(s // tb,),
        in_specs=[
            pl.BlockSpec((tb,), lambda i: (i,), memory_space=pltpu.SMEM),
            pl.BlockSpec((tb,), lambda i: (i,), memory_space=pltpu.SMEM),
            pl.BlockSpec((tb, D_MODEL), lambda i: (i, 0)),
            pl.BlockSpec((1, D_MODEL), lambda i: (0, 0)),
            pl.BlockSpec(memory_space=pl.ANY),
        ],
        out_specs=pl.BlockSpec(memory_space=pl.ANY),
        out_shape=jax.ShapeDtypeStruct((n_rows, D_MODEL), f32),
        input_output_aliases={4: 0},
        scratch_shapes=[pltpu.VMEM((2, tb, D_MODEL), f32), pltpu.SemaphoreType.DMA((2,))],
        compiler_params=_cparams(1),
        name="dispatch",
    )(pos1, pos2, h, g2, jnp.zeros((n_rows, D_MODEL), f32))


def _experts_body(te_ref, nv_ref, xs_ref, wg_ref, wu_ref, wd_ref, y_ref, wg_sc, wu_sc, wd_sc):
    i = pl.program_id(0)

    @pl.when(i < nv_ref[0])
    def _():
        prev = te_ref[jnp.maximum(i - 1, 0)]

        @pl.when((i == 0) | (te_ref[i] != prev))
        def _():
            wg_sc[...] = wg_ref[0].astype(bf16)
            wu_sc[...] = wu_ref[0].astype(bf16)
            wd_sc[...] = wd_ref[0].astype(bf16)

        x = xs_ref[...].astype(bf16)
        g = jnp.dot(x, wg_sc[...], preferred_element_type=f32)
        u = jnp.dot(x, wu_sc[...], preferred_element_type=f32)
        hm = (g * jax.nn.sigmoid(g) * u).astype(bf16)
        y_ref[...] = jnp.dot(hm, wd_sc[...], preferred_element_type=f32)

    @pl.when(i >= nv_ref[0])
    def _():
        y_ref[...] = jnp.zeros(y_ref.shape, f32)


def _experts(tile_expert, n_valid, xs, w_gate, w_up, w_down):
    n_rows = xs.shape[0]
    tm = TM_EXP
    nt = n_rows // tm
    rows = lambda i, te, nv: (jnp.minimum(i, nv[0] - 1), 0)
    wsel = lambda i, te, nv: (te[i], 0, 0)
    grid_spec = pltpu.PrefetchScalarGridSpec(
        num_scalar_prefetch=2,
        grid=(nt,),
        in_specs=[
            pl.BlockSpec((tm, D_MODEL), rows),
            pl.BlockSpec((1, D_MODEL, D_EXPERT), wsel),
            pl.BlockSpec((1, D_MODEL, D_EXPERT), wsel),
            pl.BlockSpec((1, D_EXPERT, D_MODEL), wsel),
        ],
        out_specs=pl.BlockSpec((tm, D_MODEL), lambda i, te, nv: (i, 0)),
        scratch_shapes=[
            pltpu.VMEM((D_MODEL, D_EXPERT), bf16),
            pltpu.VMEM((D_MODEL, D_EXPERT), bf16),
            pltpu.VMEM((D_EXPERT, D_MODEL), bf16),
        ],
    )
    return pl.pallas_call(
        _experts_body,
        grid_spec=grid_spec,
        out_shape=jax.ShapeDtypeStruct((n_rows, D_MODEL), f32),
        compiler_params=_cparams(1),
        name="experts",
    )(tile_expert, n_valid, xs, w_gate, w_up, w_down)


def _combine_body(p1_ref, p2_ref, p1n_ref, p2n_ref, h_ref, r_ref, y_ref, o_ref, ybuf, sems):
    tb = TB_ROWS
    i = pl.program_id(0)
    n = pl.num_programs(0)
    slot = i % 2

    def issue_block(pa_ref, pb_ref, sl):
        for r in range(tb):
            _row_copy(y_ref.at[pl.ds(pa_ref[r], 1)], ybuf.at[sl, 0, pl.ds(r, 1)], sems.at[sl]).start()
            _row_copy(y_ref.at[pl.ds(pb_ref[r], 1)], ybuf.at[sl, 1, pl.ds(r, 1)], sems.at[sl]).start()

    @pl.when(i == 0)
    def _():
        issue_block(p1_ref, p2_ref, 0)

    @pl.when(i + 1 < n)
    def _():
        issue_block(p1n_ref, p2n_ref, 1 - slot)

    for k in range(2):
        _row_copy(y_ref.at[pl.ds(0, tb)], ybuf.at[slot, k], sems.at[slot]).wait()

    r = r_ref[...]
    o_ref[...] = h_ref[...] + r[:, 2:3] * ybuf[slot, 0] + r[:, 3:4] * ybuf[slot, 1]


def _combine(pos1, pos2, h, r, y):
    s = h.shape[0]
    tb = TB_ROWS
    nb = s // tb
    cur = lambda i: (i,)
    nxt = lambda i: (jnp.minimum(i + 1, nb - 1),)
    smem = functools.partial(pl.BlockSpec, (tb,), memory_space=pltpu.SMEM)
    return pl.pallas_call(
        _combine_body,
        grid=(nb,),
        in_specs=[
            smem(cur), smem(cur), smem(nxt), smem(nxt),
            pl.BlockSpec((tb, D_MODEL), lambda i: (i, 0)),
            pl.BlockSpec((tb, LANES), lambda i: (i, 0)),
            pl.BlockSpec(memory_space=pl.ANY),
        ],
        out_specs=pl.BlockSpec((tb, D_MODEL), lambda i: (i, 0)),
        out_shape=jax.ShapeDtypeStruct((s, D_MODEL), f32),
        scratch_shapes=[pltpu.VMEM((2, 2, tb, D_MODEL), f32), pltpu.SemaphoreType.DMA((2,))],
        compiler_params=_cparams(1),
        name="combine",
    )(pos1, pos2, pos1, pos2, h, r, y)


def kernel(x, norm1_g, w_in, conv_dw_kernel, conv_dw_bias, conv_ln_g, conv_ln_b, q_norm_g, k_norm_g,
           lambda_q1, lambda_k1, lambda_q2, lambda_k2, subln_g, w_out, norm2_g, w_group, b_group,
           w_router, b_router, w_gate, w_up, w_down):
    b, s, d = x.shape
    assert b == 1 and d == D_MODEL and norm1_g.shape[0] == 1
    x2 = x.reshape(s, d)

    w_in_bf = w_in[0].astype(bf16)
    w_out_bf = w_out[0].astype(bf16)
    qg = jnp.tile(q_norm_g[0].reshape(1, 2 * HEAD_DIM), (1, N_HEADS)) * (HEAD_DIM ** -0.5 * math.log2(math.e))
    kg = jnp.tile(k_norm_g[0].reshape(1, 2 * HEAD_DIM), (1, N_HEADS))
    blk = jnp.arange(256) // HEAD_DIM
    bd = jnp.where(blk[:, None] == blk[None, :], 1.0 / HEAD_DIM, 0.0).astype(bf16)
    n_r = N_GROUPS + N_EXPERTS
    wr = jnp.concatenate(
        [w_group[0], jnp.transpose(w_router[0], (1, 0, 2)).reshape(d, N_EXPERTS),
         jnp.zeros((d, LANES - n_r), f32)], axis=1)
    wr_hi = wr.astype(bf16)
    wr = jnp.concatenate([wr_hi, (wr - wr_hi.astype(f32)).astype(bf16)], axis=1)
    br = jnp.concatenate([b_group[0], b_router[0].reshape(N_EXPERTS), jnp.zeros((LANES - n_r,), f32)])[None]
    row = lambda a: a.reshape(1, -1)

    hglu, qt, k, vt = _inproj(x2, norm1_g, w_in_bf, qg, kg, bd)
    y_conv = _conv(hglu, conv_dw_kernel[0], row(conv_dw_bias[0]), row(conv_ln_g[0]), row(conv_ln_b[0]))
    y_attn = _attn(qt, k, vt, row(lambda_q1[0]), row(lambda_k1[0]), row(lambda_q2[0]), row(lambda_k2[0]),
                   row(subln_g[0]))
    h, r = _outproj(y_conv, y_attn, x2, w_out_bf, norm2_g, wr, br)

    pos, cnt = _rank(r)
    pos1 = pos[0]
    pos2 = pos[1]
    tm = TM_EXP
    counts = cnt[0, :N_EXPERTS].astype(jnp.int32)
    padded = ((counts + tm - 1) // tm) * tm
    ends = jnp.cumsum(padded)
    n_rows = TOP_K * s + N_EXPERTS * tm
    nt = n_rows // tm
    n_valid = (ends[-1] // tm).astype(jnp.int32)
    tile_start = jnp.minimum(jnp.arange(nt, dtype=jnp.int32), n_valid - 1) * tm
    tile_expert = jnp.sum(tile_start[:, None] >= ends[None, :], axis=1).astype(jnp.int32)

    xs = _dispatch(pos1, pos2, h, norm2_g, n_rows)
    ys = _experts(tile_expert, n_valid.reshape(1), xs, w_gate[0], w_up[0], w_down[0])
    out = _combine(pos1, pos2, h, r, ys)
    return out.reshape(b, s, d)
```

```python
import functools
import math

import jax
import jax.numpy as jnp
from jax import lax
from jax.experimental import pallas as pl
from jax.experimental.pallas import tpu as pltpu

D_MODEL = 2048
D_CONV = 1024
N_HEADS = 8
HEAD_DIM = 64
V_DIM = 128
CONV_WIDTH = 31
CHUNK = 64
N_GROUPS = 4
EXPERTS_PER_GROUP = 8
N_EXPERTS = 32
TOP_K = 2
D_EXPERT = 512
D_Q = 1024
D_V = 1024
D_IN = 2 * D_CONV + 2 * D_Q + D_V
EPS = 1e-6
LAMBDA_INIT = 0.8 - 0.6 * math.exp(-0.3 * 0)

LANES = 128
SUBLANES = 8
VMEM_LIMIT = 56 * 1024 * 1024
NEG_BIG = -1e30

TM_PROJ = 512
TK_ATTN = 512
TQ_ATTN = 512
RC_CONV = 64
HALO = 32
TB_RANK = 512
TB_ROWS = 256
TM_EXP = 256

bf16 = jnp.bfloat16
f32 = jnp.float32


def _cparams(n_axes):
    return pltpu.CompilerParams(
        dimension_semantics=("arbitrary",) * n_axes, vmem_limit_bytes=VMEM_LIMIT)


def _inproj_body(x_ref, g1_ref, w_ref, qg_ref, kg_ref, bd_ref, cw_ref, cb_ref, lg_ref, lb_ref,
                 yc_ref, qt_ref, k_ref, vt_ref, pad_sc, acc_sc):
    i = pl.program_id(0)
    tm, rc = TM_PROJ, RC_CONV

    @pl.when(i == 0)
    def _():
        pad_sc[0:HALO, :] = jnp.zeros((HALO, D_CONV), f32)

    pad_sc[HALO + tm:, :] = jnp.zeros((SUBLANES, D_CONV), f32)

    x = x_ref[...]
    ms = jnp.mean(x * x, axis=-1, keepdims=True)
    xn = (x * lax.rsqrt(ms + EPS) * g1_ref[...]).astype(bf16)

    def mm(c0, n):
        return jnp.dot(xn, w_ref[:, c0:c0 + n], preferred_element_type=f32)

    nt = 512
    for c in range(0, D_CONV, nt):
        a = mm(c, nt)
        b = mm(D_CONV + c, nt)
        pad_sc[HALO:HALO + tm, c:c + nt] = a * jax.nn.sigmoid(b)

    bd = bd_ref[...]
    sl = 256

    def qk_norm(u, gain):
        ms64 = jnp.dot((u * u).astype(bf16), bd, preferred_element_type=f32)
        return u * lax.rsqrt(ms64 + EPS) * gain

    uq = mm(2 * D_CONV, D_Q)
    uk = mm(2 * D_CONV + D_Q, D_Q)
    uv = mm(2 * D_CONV + 2 * D_Q, D_V)
    for c in range(0, D_Q, sl):
        qn = qk_norm(uq[:, c:c + sl], qg_ref[:, c:c + sl])
        qt_ref[c:c + sl, :] = qn.T.astype(bf16)
    for c in range(0, D_Q, sl):
        k_ref[:, c:c + sl] = qk_norm(uk[:, c:c + sl], kg_ref[:, c:c + sl]).astype(bf16)
    for c in range(0, D_V, sl):
        vt_ref[0, c:c + sl, :] = uv[:, c:c + sl].T.astype(bf16)

    off = HALO - (CONV_WIDTH - 1)
    for c0 in range(0, D_CONV, LANES):
        cols = slice(c0, c0 + LANES)
        for r0 in range(0, tm, rc):
            acc = jnp.broadcast_to(cb_ref[:, cols], (rc, LANES))
            for a in range(SUBLANES):
                qa = None
                for jp in range(a, off + CONV_WIDTH, SUBLANES):
                    j = jp - off
                    if j < 0:
                        continue
                    base = r0 + jp - a
                    term = cw_ref[j:j + 1, cols] * pad_sc[base:base + rc + SUBLANES, cols]
                    qa = term if qa is None else qa + term
                acc = acc + qa[a:a + rc, :]
            acc_sc[r0:r0 + rc, cols] = acc
    pad_sc[0:HALO, :] = pad_sc[tm:tm + HALO, :]
    acc = acc_sc[...]
    mu = jnp.mean(acc, axis=-1, keepdims=True)
    xc = acc - mu
    var = jnp.mean(xc * xc, axis=-1, keepdims=True)
    y = xc * lax.rsqrt(var + EPS) * lg_ref[...] + lb_ref[...]
    yc_ref[...] = (y * jax.nn.sigmoid(y)).astype(bf16)


def _inproj(x2, g1, w_bf, qg, kg, bd, cw, cb, lg, lb):
    s = x2.shape[0]
    tm = TM_PROJ
    assert tm == TK_ATTN and s % tm == 0 and tm % RC_CONV == 0 and HALO >= CONV_WIDTH - 1
    const = lambda i: (0, 0)
    return pl.pallas_call(
        _inproj_body,
        grid=(s // tm,),
        in_specs=[
            pl.BlockSpec((tm, D_MODEL), lambda i: (i, 0)),
            pl.BlockSpec((1, D_MODEL), const),
            pl.BlockSpec((D_MODEL, D_IN), const, pipeline_mode=pl.Buffered(1)),
            pl.BlockSpec((1, D_Q), const),
            pl.BlockSpec((1, D_Q), const),
            pl.BlockSpec((256, 256), const),
            pl.BlockSpec((CONV_WIDTH, D_CONV), const),
            pl.BlockSpec((1, D_CONV), const),
            pl.BlockSpec((1, D_CONV), const),
            pl.BlockSpec((1, D_CONV), const),
        ],
        out_specs=[
            pl.BlockSpec((tm, D_CONV), lambda i: (i, 0)),
            pl.BlockSpec((D_Q, tm), lambda i: (0, i)),
            pl.BlockSpec((tm, D_Q), lambda i: (i, 0)),
            pl.BlockSpec((1, D_V, tm), lambda i: (i, 0, 0)),
        ],
        out_shape=[
            jax.ShapeDtypeStruct((s, D_CONV), bf16),
            jax.ShapeDtypeStruct((D_Q, s), bf16),
            jax.ShapeDtypeStruct((s, D_Q), bf16),
            jax.ShapeDtypeStruct((s // tm, D_V, tm), bf16),
        ],
        scratch_shapes=[pltpu.VMEM((HALO + tm + SUBLANES, D_CONV), f32), pltpu.VMEM((tm, D_CONV), f32)],
        compiler_params=_cparams(1),
        name="inproj",
    )(x2, g1, w_bf, qg, kg, bd, cw, cb, lg, lb)


def _attn_body(qt_ref, k_ref, vt_ref, lq1_ref, lk1_ref, lq2_ref, lk2_ref, sg_ref,
               o_ref, m_sc, l_sc, acc_sc, sa_sc, sb_sc):
    tq, tk = TQ_ATTN, TK_ATTN
    i = pl.program_id(1)
    qt = qt_ref[...]
    sub = lax.broadcasted_iota(jnp.int32, qt.shape, 0)
    zero = jnp.zeros_like(qt)
    qst = jnp.concatenate([jnp.where(sub < HEAD_DIM, qt, zero),
                           jnp.where(sub >= HEAD_DIM, qt, zero)], axis=1)

    m_sc[...] = jnp.full(m_sc.shape, NEG_BIG, f32)
    l_sc[...] = jnp.zeros(l_sc.shape, f32)
    acc_sc[...] = jnp.zeros(acc_sc.shape, f32)

    def scores(kb, dst):
        kblk = k_ref[pl.ds(pl.multiple_of(kb * tk, tk), tk), :]
        dst[...] = jnp.dot(kblk, qst, preferred_element_type=f32)

    def consume(src, kb, masked):
        s = src[...]
        if masked:
            key = lax.broadcasted_iota(jnp.int32, s.shape, 0)
            qry = lax.broadcasted_iota(jnp.int32, s.shape, 1)
            vis = (key // CHUNK) <= ((qry % tq) // CHUNK)
            s = jnp.where(vis, s, NEG_BIG)
        m_old = m_sc[...]
        m_new = jnp.maximum(m_old, jnp.max(s, axis=0, keepdims=True))
        alpha = jnp.exp2(m_old - m_new)
        p = jnp.exp2(s - m_new)
        l_sc[...] = alpha * l_sc[...] + jnp.sum(p, axis=0, keepdims=True)
        acc_sc[...] = alpha * acc_sc[...] + jnp.dot(vt_ref[kb], p.astype(bf16),
                                                    preferred_element_type=f32)
        m_sc[...] = m_new

    last_full = jnp.maximum(i - 1, 0)
    scores(i, sa_sc)
    scores(0, sb_sc)
    consume(sa_sc, i, True)

    def pair(j, carry):
        scores(2 * j + 1, sa_sc)
        consume(sb_sc, 2 * j, False)
        scores(jnp.minimum(2 * j + 2, last_full), sb_sc)
        consume(sa_sc, 2 * j + 1, False)
        return carry

    lax.fori_loop(0, i // 2, pair, 0)

    @pl.when(i % 2 == 1)
    def _():
        consume(sb_sc, i - 1, False)

    lam = (jnp.exp(jnp.sum(lq1_ref[...] * lk1_ref[...], axis=-1, keepdims=True))
           - jnp.exp(jnp.sum(lq2_ref[...] * lk2_ref[...], axis=-1, keepdims=True))
           + LAMBDA_INIT)
    o_all = acc_sc[...] / l_sc[...]
    o = (o_all[:, :tq] - lam * o_all[:, tq:]).T
    ms = jnp.mean(o * o, axis=-1, keepdims=True)
    y = o * lax.rsqrt(ms + EPS) * sg_ref[...] * (1.0 - LAMBDA_INIT)
    o_ref[...] = y.astype(bf16)


def _attn(qt, k, vt, lq1, lk1, lq2, lk2, sg):
    s = k.shape[0]
    tq, tk = TQ_ATTN, TK_ATTN
    assert tq == tk and s % tq == 0
    vec = lambda h, i: (0, 0)
    return pl.pallas_call(
        _attn_body,
        grid=(N_HEADS, s // tq),
        in_specs=[
            pl.BlockSpec((2 * HEAD_DIM, tq), lambda h, i: (h, i)),
            pl.BlockSpec((s, 2 * HEAD_DIM), lambda h, i: (0, h)),
            pl.BlockSpec((s // tk, V_DIM, tk), lambda h, i: (0, h, 0)),
            pl.BlockSpec((1, HEAD_DIM), vec),
            pl.BlockSpec((1, HEAD_DIM), vec),
            pl.BlockSpec((1, HEAD_DIM), vec),
            pl.BlockSpec((1, HEAD_DIM), vec),
            pl.BlockSpec((1, V_DIM), vec),
        ],
        out_specs=pl.BlockSpec((tq, V_DIM), lambda h, i: (i, h)),
        out_shape=jax.ShapeDtypeStruct((s, D_V), bf16),
        scratch_shapes=[
            pltpu.VMEM((1, 2 * tq), f32),
            pltpu.VMEM((1, 2 * tq), f32),
            pltpu.VMEM((V_DIM, 2 * tq), f32),
            pltpu.VMEM((tk, 2 * tq), f32),
            pltpu.VMEM((tk, 2 * tq), f32),
        ],
        compiler_params=_cparams(2),
        name="attn",
    )(qt, k, vt, lq1, lk1, lq2, lk2, sg)


def _outproj_body(yc_ref, ya_ref, x_ref, w_ref, g2_ref, wr_ref, br_ref, h_ref, r_ref):
    half = TM_PROJ // 2
    for r0 in range(0, TM_PROJ, half):
        rows = slice(r0, r0 + half)
        h = (jnp.dot(yc_ref[rows, :], w_ref[0:D_CONV, :], preferred_element_type=f32)
             + jnp.dot(ya_ref[rows, :], w_ref[D_CONV:, :], preferred_element_type=f32)
             + x_ref[rows, :])
        h_ref[rows, :] = h
        ms = jnp.mean(h * h, axis=-1, keepdims=True)
        t = h * lax.rsqrt(ms + EPS) * g2_ref[...]
        t_hi = t.astype(bf16)
        t_lo = (t - t_hi.astype(f32)).astype(bf16)
        hh = jnp.dot(t_hi, wr_ref[...], preferred_element_type=f32)
        lh = jnp.dot(t_lo, wr_ref[:, 0:LANES], preferred_element_type=f32)
        logits = hh[:, 0:LANES] + hh[:, LANES:] + lh + br_ref[...]
        r_ref[rows, :] = _route(logits)


def _route(logits):
    lane = lax.broadcasted_iota(jnp.int32, logits.shape, 1)
    lanef = lane.astype(f32)
    big = float(LANES)
    is_g = lane < N_GROUPS
    gl = jnp.where(is_g, logits, NEG_BIG)
    gmax = jnp.max(gl, axis=1, keepdims=True)
    gidx = jnp.min(jnp.where(gl == gmax, lanef, big), axis=1, keepdims=True)
    gsum = jnp.sum(jnp.where(is_g, jnp.exp(gl - gmax), 0.0), axis=1, keepdims=True)
    gw = 1.0 / gsum

    e_of_lane = lane - N_GROUPS
    in_group = ((e_of_lane >= 0) & (e_of_lane < N_EXPERTS)
                & ((e_of_lane // EXPERTS_PER_GROUP).astype(f32) == gidx))
    el = jnp.where(in_group, logits, NEG_BIG)
    v1 = jnp.max(el, axis=1, keepdims=True)
    i1 = jnp.min(jnp.where(el == v1, lanef, big), axis=1, keepdims=True)
    el2 = jnp.where(lanef == i1, NEG_BIG, el)
    v2 = jnp.max(el2, axis=1, keepdims=True)
    i2 = jnp.min(jnp.where(el2 == v2, lanef, big), axis=1, keepdims=True)
    e21 = jnp.exp(v2 - v1)
    den = 1.0 + e21
    w1 = gw * (1.0 / den)
    w2 = gw * (e21 / den)
    return jnp.where(lane == 0, i1 - N_GROUPS,
                     jnp.where(lane == 1, i2 - N_GROUPS,
                               jnp.where(lane == 2, w1, jnp.where(lane == 3, w2, 0.0))))


def _outproj(yc, ya, x2, w_bf, g2, wr, br):
    s = x2.shape[0]
    tm = TM_PROJ
    const = lambda i: (0, 0)
    row = lambda i: (i, 0)
    return pl.pallas_call(
        _outproj_body,
        grid=(s // tm,),
        in_specs=[
            pl.BlockSpec((tm, D_CONV), row),
            pl.BlockSpec((tm, D_V), row),
            pl.BlockSpec((tm, D_MODEL), row),
            pl.BlockSpec((D_CONV + D_V, D_MODEL), const, pipeline_mode=pl.Buffered(1)),
            pl.BlockSpec((1, D_MODEL), const),
            pl.BlockSpec((D_MODEL, 2 * LANES), const),
            pl.BlockSpec((1, LANES), const),
        ],
        out_specs=[pl.BlockSpec((tm, D_MODEL), row), pl.BlockSpec((tm, LANES), row)],
        out_shape=[jax.ShapeDtypeStruct((s, D_MODEL), f32), jax.ShapeDtypeStruct((s, LANES), f32)],
        compiler_params=_cparams(1),
        name="outproj",
    )(yc, ya, x2, w_bf, g2, wr, br)


def _rank_body(r_ref, pos_ref, cnt_ref, run_sc, rank_sc):
    tb = TB_RANK
    ph = pl.program_id(0)
    i = pl.program_id(1)
    r = r_ref[...]
    lane = lax.broadcasted_iota(jnp.int32, r.shape, 1)
    lanef = lane.astype(f32)
    oh1 = (lanef == r[:, 0:1]).astype(f32)
    oh2 = (lanef == r[:, 1:2]).astype(f32)

    @pl.when((ph == 0) & (i == 0))
    def _():
        run_sc[...] = jnp.zeros(run_sc.shape, f32)

    @pl.when(ph == 0)
    def _():
        both = oh1 + oh2
        row = lax.broadcasted_iota(jnp.int32, (tb, tb), 0)
        col = lax.broadcasted_iota(jnp.int32, (tb, tb), 1)
        tri = (row > col).astype(bf16)
        before = jnp.dot(tri, both.astype(bf16), preferred_element_type=f32) + run_sc[...]
        rank1 = jnp.sum(before * oh1, axis=1, keepdims=True)
        rank2 = jnp.sum(before * oh2, axis=1, keepdims=True)
        rank_sc[i] = jnp.where(lane == 0, rank1, jnp.where(lane == 1, rank2, 0.0))
        run_sc[...] = run_sc[...] + jnp.sum(both, axis=0, keepdims=True)

    @pl.when(ph == 1)
    def _():
        cnt = run_sc[...]
        tiles = jnp.floor((cnt + (TM_EXP - 1)) * (1.0 / TM_EXP))
        lr = lax.broadcasted_iota(jnp.int32, (LANES, LANES), 0)
        lc = lax.broadcasted_iota(jnp.int32, (LANES, LANES), 1)
        upper = (lr < lc).astype(bf16)
        tiles8 = jnp.broadcast_to(tiles, (SUBLANES, LANES)).astype(bf16)
        offs = jnp.dot(tiles8, upper, preferred_element_type=f32)[0:1, :] * TM_EXP
        rk = rank_sc[i]
        pos1 = jnp.sum(oh1 * offs, axis=1, keepdims=True) + rk[:, 0:1]
        pos2 = jnp.sum(oh2 * offs, axis=1, keepdims=True) + rk[:, 1:2]
        pp = jnp.where(lane == 0, pos1, jnp.where(lane == 1, pos2, 0.0))
        pos_ref[...] = pp.T[0:SUBLANES, :].astype(jnp.int32)
        cnt_ref[...] = cnt


def _rank(r):
    s = r.shape[0]
    tb = TB_RANK
    nb = s // tb
    return pl.pallas_call(
        _rank_body,
        grid=(2, nb),
        in_specs=[pl.BlockSpec((tb, LANES), lambda ph, i: (i, 0))],
        out_specs=[pl.BlockSpec((SUBLANES, tb), lambda ph, i: (0, i * ph)),
                   pl.BlockSpec((1, LANES), lambda ph, i: (0, 0))],
        out_shape=[jax.ShapeDtypeStruct((SUBLANES, s), jnp.int32), jax.ShapeDtypeStruct((1, LANES), f32)],
        scratch_shapes=[pltpu.VMEM((1, LANES), f32), pltpu.VMEM((nb, tb, LANES), f32)],
        compiler_params=_cparams(2),
        name="rank",
    )(r)


def _row_copy(src, dst, sem):
    return pltpu.make_async_copy(src, dst, sem)


def _dispatch_body(ends_ref, nv_ref, pos1_ref, pos2_ref, h_ref, g2_ref, xs_ref, tbuf, zbuf, sems, zsem):
    tb, tm = TB_ROWS, TM_EXP
    i = pl.program_id(0)
    n = pl.num_programs(0)
    slot = i % 2

    def wait_slot(sl):
        for _ in range(2):
            _row_copy(tbuf.at[sl], xs_ref.at[pl.ds(0, tb)], sems.at[sl]).wait()

    @pl.when(i == 0)
    def _():
        zbuf[...] = jnp.zeros(zbuf.shape, f32)

        def zero_tile(t):
            return _row_copy(zbuf, xs_ref.at[pl.ds(pl.multiple_of(t * tm, tm), tm)], zsem)

        def each_fill(fn):
            for e in range(N_EXPERTS):
                first = 0 if e == 0 else ends_ref[e - 1]

                @pl.when(ends_ref[e] > first)
                def _():
                    fn(lax.shift_right_logical(ends_ref[e], tm.bit_length() - 1) - 1)

            def tail(t, carry):
                fn(t)
                return carry

            lax.fori_loop(nv_ref[0], xs_ref.shape[0] // tm, tail, 0)

        each_fill(lambda t: zero_tile(t).start())
        each_fill(lambda t: zero_tile(t).wait())

    @pl.when(i >= 2)
    def _():
        wait_slot(slot)

    h = h_ref[...]
    ms = jnp.mean(h * h, axis=-1, keepdims=True)
    tbuf[slot] = h * lax.rsqrt(ms + EPS) * g2_ref[...]

    for r in range(tb):
        src = tbuf.at[slot, pl.ds(r, 1)]
        _row_copy(src, xs_ref.at[pl.ds(pos1_ref[r], 1)], sems.at[slot]).start()
        _row_copy(src, xs_ref.at[pl.ds(pos2_ref[r], 1)], sems.at[slot]).start()

    @pl.when(i == n - 1)
    def _():
        wait_slot(slot)

        @pl.when(n >= 2)
        def _():
            wait_slot(1 - slot)


def _dispatch(ends, n_valid, pos1, pos2, h, g2, n_rows):
    s = h.shape[0]
    tb, tm = TB_ROWS, TM_EXP
    assert tm & (tm - 1) == 0 and n_rows % tm == 0
    grid_spec = pltpu.PrefetchScalarGridSpec(
        num_scalar_prefetch=2,
        grid=(s // tb,),
        in_specs=[
            pl.BlockSpec((tb,), lambda i, en, nv: (i,), memory_space=pltpu.SMEM),
            pl.BlockSpec((tb,), lambda i, en, nv: (i,), memory_space=pltpu.SMEM),
            pl.BlockSpec((tb, D_MODEL), lambda i, en, nv: (i, 0)),
            pl.BlockSpec((1, D_MODEL), lambda i, en, nv: (0, 0)),
        ],
        out_specs=pl.BlockSpec(memory_space=pl.ANY),
        scratch_shapes=[
            pltpu.VMEM((2, tb, D_MODEL), f32),
            pltpu.VMEM((tm, D_MODEL), f32),
            pltpu.SemaphoreType.DMA((2,)),
            pltpu.SemaphoreType.DMA,
        ],
    )
    return pl.pallas_call(
        _dispatch_body,
        grid_spec=grid_spec,
        out_shape=jax.ShapeDtypeStruct((n_rows, D_MODEL), f32),
        compiler_params=_cparams(1),
        name="dispatch",
    )(ends, n_valid, pos1, pos2, h, g2)


def _experts_body(te_ref, nv_ref, xs_ref, wg_ref, wu_ref, wd_ref, y_ref, wg_sc, wu_sc, wd_sc):
    i = pl.program_id(0)

    @pl.when(i < nv_ref[0])
    def _():
        prev = te_ref[jnp.maximum(i - 1, 0)]

        @pl.when((i == 0) | (te_ref[i] != prev))
        def _():
            wg_sc[...] = wg_ref[0].astype(bf16)
            wu_sc[...] = wu_ref[0].astype(bf16)
            wd_sc[...] = wd_ref[0].astype(bf16)

        x = xs_ref[...].astype(bf16)
        g = jnp.dot(x, wg_sc[...], preferred_element_type=f32)
        u = jnp.dot(x, wu_sc[...], preferred_element_type=f32)
        hm = (g * jax.nn.sigmoid(g) * u).astype(bf16)
        y_ref[...] = jnp.dot(hm, wd_sc[...], preferred_element_type=f32)

    @pl.when(i >= nv_ref[0])
    def _():
        y_ref[...] = jnp.zeros(y_ref.shape, f32)


def _experts(tile_expert, n_valid, xs, w_gate, w_up, w_down):
    n_rows = xs.shape[0]
    tm = TM_EXP
    nt = n_rows // tm
    rows = lambda i, te, nv: (jnp.minimum(i, nv[0] - 1), 0)
    wsel = lambda i, te, nv: (te[i], 0, 0)
    grid_spec = pltpu.PrefetchScalarGridSpec(
        num_scalar_prefetch=2,
        grid=(nt,),
        in_specs=[
            pl.BlockSpec((tm, D_MODEL), rows),
            pl.BlockSpec((1, D_MODEL, D_EXPERT), wsel),
            pl.BlockSpec((1, D_MODEL, D_EXPERT), wsel),
            pl.BlockSpec((1, D_EXPERT, D_MODEL), wsel),
        ],
        out_specs=pl.BlockSpec((tm, D_MODEL), lambda i, te, nv: (i, 0)),
        scratch_shapes=[
            pltpu.VMEM((D_MODEL, D_EXPERT), bf16),
            pltpu.VMEM((D_MODEL, D_EXPERT), bf16),
            pltpu.VMEM((D_EXPERT, D_MODEL), bf16),
        ],
    )
    return pl.pallas_call(
        _experts_body,
        grid_spec=grid_spec,
        out_shape=jax.ShapeDtypeStruct((n_rows, D_MODEL), f32),
        compiler_params=_cparams(1),
        name="experts",
    )(tile_expert, n_valid, xs, w_gate, w_up, w_down)


def _combine_body(p1_ref, p2_ref, p1n_ref, p2n_ref, h_ref, r_ref, y_ref, o_ref, ybuf, sems):
    tb = TB_ROWS
    i = pl.program_id(0)
    n = pl.num_programs(0)
    slot = i % 2

    def issue_block(pa_ref, pb_ref, sl):
        for r in range(tb):
            _row_copy(y_ref.at[pl.ds(pa_ref[r], 1)], ybuf.at[sl, 0, pl.ds(r, 1)], sems.at[sl]).start()
            _row_copy(y_ref.at[pl.ds(pb_ref[r], 1)], ybuf.at[sl, 1, pl.ds(r, 1)], sems.at[sl]).start()

    @pl.when(i == 0)
    def _():
        issue_block(p1_ref, p2_ref, 0)

    @pl.when(i + 1 < n)
    def _():
        issue_block(p1n_ref, p2n_ref, 1 - slot)

    for k in range(2):
        _row_copy(y_ref.at[pl.ds(0, tb)], ybuf.at[slot, k], sems.at[slot]).wait()

    r = r_ref[...]
    o_ref[...] = h_ref[...] + r[:, 2:3] * ybuf[slot, 0] + r[:, 3:4] * ybuf[slot, 1]


def _combine(pos1, pos2, h, r, y):
    s = h.shape[0]
    tb = TB_ROWS
    nb = s // tb
    cur = lambda i: (i,)
    nxt = lambda i: (jnp.minimum(i + 1, nb - 1),)
    smem = functools.partial(pl.BlockSpec, (tb,), memory_space=pltpu.SMEM)
    return pl.pallas_call(
        _combine_body,
        grid=(nb,),
        in_specs=[
            smem(cur), smem(cur), smem(nxt), smem(nxt),
            pl.BlockSpec((tb, D_MODEL), lambda i: (i, 0)),
            pl.BlockSpec((tb, LANES), lambda i: (i, 0)),
            pl.BlockSpec(memory_space=pl.ANY),
        ],
        out_specs=pl.BlockSpec((tb, D_MODEL), lambda i: (i, 0)),
        out_shape=jax.ShapeDtypeStruct((s, D_MODEL), f32),
        scratch_shapes=[pltpu.VMEM((2, 2, tb, D_MODEL), f32), pltpu.SemaphoreType.DMA((2,))],
        compiler_params=_cparams(1),
        name="combine",
    )(pos1, pos2, pos1, pos2, h, r, y)


def kernel(x, norm1_g, w_in, conv_dw_kernel, conv_dw_bias, conv_ln_g, conv_ln_b, q_norm_g, k_norm_g,
           lambda_q1, lambda_k1, lambda_q2, lambda_k2, subln_g, w_out, norm2_g, w_group, b_group,
           w_router, b_router, w_gate, w_up, w_down):
    b, s, d = x.shape
    assert b == 1 and d == D_MODEL and norm1_g.shape[0] == 1
    x2 = x.reshape(s, d)

    w_in_bf = w_in[0].astype(bf16)
    w_out_bf = w_out[0].astype(bf16)
    qg = jnp.tile(q_norm_g[0].reshape(1, 2 * HEAD_DIM), (1, N_HEADS)) * (HEAD_DIM ** -0.5 * math.log2(math.e))
    kg = jnp.tile(k_norm_g[0].reshape(1, 2 * HEAD_DIM), (1, N_HEADS))
    blk = jnp.arange(256) // HEAD_DIM
    bd = jnp.where(blk[:, None] == blk[None, :], 1.0 / HEAD_DIM, 0.0).astype(bf16)
    n_r = N_GROUPS + N_EXPERTS
    wr = jnp.concatenate(
        [w_group[0], jnp.transpose(w_router[0], (1, 0, 2)).reshape(d, N_EXPERTS),
         jnp.zeros((d, LANES - n_r), f32)], axis=1)
    wr_hi = wr.astype(bf16)
    wr = jnp.concatenate([wr_hi, (wr - wr_hi.astype(f32)).astype(bf16)], axis=1)
    br = jnp.concatenate([b_group[0], b_router[0].reshape(N_EXPERTS), jnp.zeros((LANES - n_r,), f32)])[None]
    row = lambda a: a.reshape(1, -1)

    y_conv, qt, k, vt = _inproj(x2, norm1_g, w_in_bf, qg, kg, bd, conv_dw_kernel[0], row(conv_dw_bias[0]),
                                row(conv_ln_g[0]), row(conv_ln_b[0]))
    y_attn = _attn(qt, k, vt, row(lambda_q1[0]), row(lambda_k1[0]), row(lambda_q2[0]), row(lambda_k2[0]),
                   row(subln_g[0]))
    h, r = _outproj(y_conv, y_attn, x2, w_out_bf, norm2_g, wr, br)

    pos, cnt = _rank(r)
    pos1 = pos[0]
    pos2 = pos[1]
    tm = TM_EXP
    counts = cnt[0, :N_EXPERTS].astype(jnp.int32)
    padded = ((counts + tm - 1) // tm) * tm
    ends = jnp.cumsum(padded)
    n_rows = TOP_K * s + N_EXPERTS * tm
    nt = n_rows // tm
    n_valid = (ends[-1] // tm).astype(jnp.int32)
    tile_start = jnp.minimum(jnp.arange(nt, dtype=jnp.int32), n_valid - 1) * tm
    tile_expert = jnp.sum(tile_start[:, None] >= ends[None, :], axis=1).astype(jnp.int32)

    n_valid = n_valid.reshape(1)
    xs = _dispatch(ends.astype(jnp.int32), n_valid, pos1, pos2, h, norm2_g, n_rows)
    ys = _experts(tile_expert, n_valid, xs, w_gate[0], w_up[0], w_down[0])
    out = _combine(pos1, pos2, h, r, ys)
    return out.reshape(b, s, d)
```

```python
import functools
import math

import jax
import jax.numpy as jnp
from jax import lax
from jax.experimental import pallas as pl
from jax.experimental.pallas import tpu as pltpu

D_MODEL = 2048
D_CONV = 1024
N_HEADS = 8
HEAD_DIM = 64
V_DIM = 128
CONV_WIDTH = 31
CHUNK = 64
N_GROUPS = 4
EXPERTS_PER_GROUP = 8
N_EXPERTS = 32
TOP_K = 2
D_EXPERT = 512
D_Q = 1024
D_V = 1024
D_IN = 2 * D_CONV + 2 * D_Q + D_V
EPS = 1e-6
LAMBDA_INIT = 0.8 - 0.6 * math.exp(-0.3 * 0)

LANES = 128
SUBLANES = 8
VMEM_LIMIT = 56 * 1024 * 1024
NEG_BIG = -1e30

TM_PROJ = 512
TK_ATTN = 512
TQ_ATTN = 512
RC_CONV = 64
HALO = 32
TB_RANK = 512
TB_ROWS = 256
TM_EXP = 256

bf16 = jnp.bfloat16
f32 = jnp.float32


def _cparams(n_axes):
    return pltpu.CompilerParams(
        dimension_semantics=("arbitrary",) * n_axes, vmem_limit_bytes=VMEM_LIMIT)


def _inproj_body(x_ref, g1_ref, w_ref, qg_ref, kg_ref, bd_ref, cw_ref, cb_ref, lg_ref, lb_ref,
                 yc_ref, qt_ref, k_ref, vt_ref, pad_sc, acc_sc):
    i = pl.program_id(0)
    tm, rc = TM_PROJ, RC_CONV

    @pl.when(i == 0)
    def _():
        pad_sc[0:HALO, :] = jnp.zeros((HALO, D_CONV), f32)

    pad_sc[HALO + tm:, :] = jnp.zeros((SUBLANES, D_CONV), f32)

    x = x_ref[...]
    ms = jnp.mean(x * x, axis=-1, keepdims=True)
    xn = (x * lax.rsqrt(ms + EPS) * g1_ref[...]).astype(bf16)

    def mm(c0, n):
        return jnp.dot(xn, w_ref[:, c0:c0 + n], preferred_element_type=f32)

    nt = 512
    for c in range(0, D_CONV, nt):
        a = mm(c, nt)
        b = mm(D_CONV + c, nt)
        pad_sc[HALO:HALO + tm, c:c + nt] = a * jax.nn.sigmoid(b)

    bd = bd_ref[...]
    sl = 256

    def qk_norm(u, gain):
        ms64 = jnp.dot((u * u).astype(bf16), bd, preferred_element_type=f32)
        return u * lax.rsqrt(ms64 + EPS) * gain

    uq = mm(2 * D_CONV, D_Q)
    uk = mm(2 * D_CONV + D_Q, D_Q)
    uv = mm(2 * D_CONV + 2 * D_Q, D_V)
    for c in range(0, D_Q, sl):
        qn = qk_norm(uq[:, c:c + sl], qg_ref[:, c:c + sl])
        qt_ref[c:c + sl, :] = qn.T.astype(bf16)
    for c in range(0, D_Q, sl):
        k_ref[:, c:c + sl] = qk_norm(uk[:, c:c + sl], kg_ref[:, c:c + sl]).astype(bf16)
    for c in range(0, D_V, sl):
        vt_ref[0, c:c + sl, :] = uv[:, c:c + sl].T.astype(bf16)

    off = HALO - (CONV_WIDTH - 1)
    for c0 in range(0, D_CONV, LANES):
        cols = slice(c0, c0 + LANES)
        for r0 in range(0, tm, rc):
            acc = jnp.broadcast_to(cb_ref[:, cols], (rc, LANES))
            for a in range(SUBLANES):
                qa = None
                for jp in range(a, off + CONV_WIDTH, SUBLANES):
                    j = jp - off
                    if j < 0:
                        continue
                    base = r0 + jp - a
                    term = cw_ref[j:j + 1, cols] * pad_sc[base:base + rc + SUBLANES, cols]
                    qa = term if qa is None else qa + term
                acc = acc + qa[a:a + rc, :]
            acc_sc[r0:r0 + rc, cols] = acc
    pad_sc[0:HALO, :] = pad_sc[tm:tm + HALO, :]
    acc = acc_sc[...]
    mu = jnp.mean(acc, axis=-1, keepdims=True)
    xc = acc - mu
    var = jnp.mean(xc * xc, axis=-1, keepdims=True)
    y = xc * lax.rsqrt(var + EPS) * lg_ref[...] + lb_ref[...]
    yc_ref[...] = (y * jax.nn.sigmoid(y)).astype(bf16)


def _inproj(x2, g1, w_bf, qg, kg, bd, cw, cb, lg, lb):
    s = x2.shape[0]
    tm = TM_PROJ
    assert tm == TK_ATTN and s % tm == 0 and tm % RC_CONV == 0 and HALO >= CONV_WIDTH - 1
    const = lambda i: (0, 0)
    return pl.pallas_call(
        _inproj_body,
        grid=(s // tm,),
        in_specs=[
            pl.BlockSpec((tm, D_MODEL), lambda i: (i, 0)),
            pl.BlockSpec((1, D_MODEL), const),
            pl.BlockSpec((D_MODEL, D_IN), const, pipeline_mode=pl.Buffered(1)),
            pl.BlockSpec((1, D_Q), const),
            pl.BlockSpec((1, D_Q), const),
            pl.BlockSpec((256, 256), const),
            pl.BlockSpec((CONV_WIDTH, D_CONV), const),
            pl.BlockSpec((1, D_CONV), const),
            pl.BlockSpec((1, D_CONV), const),
            pl.BlockSpec((1, D_CONV), const),
        ],
        out_specs=[
            pl.BlockSpec((tm, D_CONV), lambda i: (i, 0)),
            pl.BlockSpec((D_Q, tm), lambda i: (0, i)),
            pl.BlockSpec((tm, D_Q), lambda i: (i, 0)),
            pl.BlockSpec((1, D_V, tm), lambda i: (i, 0, 0)),
        ],
        out_shape=[
            jax.ShapeDtypeStruct((s, D_CONV), bf16),
            jax.ShapeDtypeStruct((D_Q, s), bf16),
            jax.ShapeDtypeStruct((s, D_Q), bf16),
            jax.ShapeDtypeStruct((s // tm, D_V, tm), bf16),
        ],
        scratch_shapes=[pltpu.VMEM((HALO + tm + SUBLANES, D_CONV), f32), pltpu.VMEM((tm, D_CONV), f32)],
        compiler_params=_cparams(1),
        name="inproj",
    )(x2, g1, w_bf, qg, kg, bd, cw, cb, lg, lb)


def _attn_body(qt_ref, k_ref, vt_ref, lq1_ref, lk1_ref, lq2_ref, lk2_ref, sg_ref,
               o_ref, m_sc, l_sc, acc_sc, sa_sc, sb_sc):
    tq, tk = TQ_ATTN, TK_ATTN
    i = pl.program_id(1)
    qt = qt_ref[...]
    sub = lax.broadcasted_iota(jnp.int32, qt.shape, 0)
    zero = jnp.zeros_like(qt)
    qst = jnp.concatenate([jnp.where(sub < HEAD_DIM, qt, zero),
                           jnp.where(sub >= HEAD_DIM, qt, zero)], axis=1)

    m_sc[...] = jnp.full(m_sc.shape, NEG_BIG, f32)
    l_sc[...] = jnp.zeros(l_sc.shape, f32)
    acc_sc[...] = jnp.zeros(acc_sc.shape, f32)

    def scores(kb, dst):
        kblk = k_ref[pl.ds(pl.multiple_of(kb * tk, tk), tk), :]
        dst[...] = jnp.dot(kblk, qst, preferred_element_type=f32)

    def consume(src, kb, masked):
        s = src[...]
        if masked:
            key = lax.broadcasted_iota(jnp.int32, s.shape, 0)
            qry = lax.broadcasted_iota(jnp.int32, s.shape, 1)
            vis = (key // CHUNK) <= ((qry % tq) // CHUNK)
            s = jnp.where(vis, s, NEG_BIG)
        m_old = m_sc[...]
        m_new = jnp.maximum(m_old, jnp.max(s, axis=0, keepdims=True))
        alpha = jnp.exp2(m_old - m_new)
        p = jnp.exp2(s - m_new)
        l_sc[...] = alpha * l_sc[...] + jnp.sum(p, axis=0, keepdims=True)
        acc_sc[...] = alpha * acc_sc[...] + jnp.dot(vt_ref[kb], p.astype(bf16),
                                                    preferred_element_type=f32)
        m_sc[...] = m_new

    last_full = jnp.maximum(i - 1, 0)
    scores(i, sa_sc)
    scores(0, sb_sc)
    consume(sa_sc, i, True)

    def pair(j, carry):
        scores(2 * j + 1, sa_sc)
        consume(sb_sc, 2 * j, False)
        scores(jnp.minimum(2 * j + 2, last_full), sb_sc)
        consume(sa_sc, 2 * j + 1, False)
        return carry

    lax.fori_loop(0, i // 2, pair, 0)

    @pl.when(i % 2 == 1)
    def _():
        consume(sb_sc, i - 1, False)

    lam = (jnp.exp(jnp.sum(lq1_ref[...] * lk1_ref[...], axis=-1, keepdims=True))
           - jnp.exp(jnp.sum(lq2_ref[...] * lk2_ref[...], axis=-1, keepdims=True))
           + LAMBDA_INIT)
    o_all = acc_sc[...] / l_sc[...]
    o = (o_all[:, :tq] - lam * o_all[:, tq:]).T
    ms = jnp.mean(o * o, axis=-1, keepdims=True)
    y = o * lax.rsqrt(ms + EPS) * sg_ref[...] * (1.0 - LAMBDA_INIT)
    o_ref[...] = y.astype(bf16)


def _attn(qt, k, vt, lq1, lk1, lq2, lk2, sg):
    s = k.shape[0]
    tq, tk = TQ_ATTN, TK_ATTN
    assert tq == tk and s % tq == 0
    vec = lambda h, i: (0, 0)
    return pl.pallas_call(
        _attn_body,
        grid=(N_HEADS, s // tq),
        in_specs=[
            pl.BlockSpec((2 * HEAD_DIM, tq), lambda h, i: (h, i)),
            pl.BlockSpec((s, 2 * HEAD_DIM), lambda h, i: (0, h)),
            pl.BlockSpec((s // tk, V_DIM, tk), lambda h, i: (0, h, 0)),
            pl.BlockSpec((1, HEAD_DIM), vec),
            pl.BlockSpec((1, HEAD_DIM), vec),
            pl.BlockSpec((1, HEAD_DIM), vec),
            pl.BlockSpec((1, HEAD_DIM), vec),
            pl.BlockSpec((1, V_DIM), vec),
        ],
        out_specs=pl.BlockSpec((tq, V_DIM), lambda h, i: (i, h)),
        out_shape=jax.ShapeDtypeStruct((s, D_V), bf16),
        scratch_shapes=[
            pltpu.VMEM((1, 2 * tq), f32),
            pltpu.VMEM((1, 2 * tq), f32),
            pltpu.VMEM((V_DIM, 2 * tq), f32),
            pltpu.VMEM((tk, 2 * tq), f32),
            pltpu.VMEM((tk, 2 * tq), f32),
        ],
        compiler_params=_cparams(2),
        name="attn",
    )(qt, k, vt, lq1, lk1, lq2, lk2, sg)


def _outproj_body(yc_ref, ya_ref, x_ref, w_ref, g2_ref, wr_ref, br_ref, h_ref, r_ref):
    half = TM_PROJ // 2
    for r0 in range(0, TM_PROJ, half):
        rows = slice(r0, r0 + half)
        h = (jnp.dot(yc_ref[rows, :], w_ref[0:D_CONV, :], preferred_element_type=f32)
             + jnp.dot(ya_ref[rows, :], w_ref[D_CONV:, :], preferred_element_type=f32)
             + x_ref[rows, :])
        h_ref[rows, :] = h
        ms = jnp.mean(h * h, axis=-1, keepdims=True)
        t = h * lax.rsqrt(ms + EPS) * g2_ref[...]
        t_hi = t.astype(bf16)
        t_lo = (t - t_hi.astype(f32)).astype(bf16)
        hh = jnp.dot(t_hi, wr_ref[...], preferred_element_type=f32)
        lh = jnp.dot(t_lo, wr_ref[:, 0:LANES], preferred_element_type=f32)
        logits = hh[:, 0:LANES] + hh[:, LANES:] + lh + br_ref[...]
        r_ref[rows, :] = _route(logits)


def _route(logits):
    lane = lax.broadcasted_iota(jnp.int32, logits.shape, 1)
    lanef = lane.astype(f32)
    big = float(LANES)
    is_g = lane < N_GROUPS
    gl = jnp.where(is_g, logits, NEG_BIG)
    gmax = jnp.max(gl, axis=1, keepdims=True)
    gidx = jnp.min(jnp.where(gl == gmax, lanef, big), axis=1, keepdims=True)
    gsum = jnp.sum(jnp.where(is_g, jnp.exp(gl - gmax), 0.0), axis=1, keepdims=True)
    gw = 1.0 / gsum

    e_of_lane = lane - N_GROUPS
    in_group = ((e_of_lane >= 0) & (e_of_lane < N_EXPERTS)
                & ((e_of_lane // EXPERTS_PER_GROUP).astype(f32) == gidx))
    el = jnp.where(in_group, logits, NEG_BIG)
    v1 = jnp.max(el, axis=1, keepdims=True)
    i1 = jnp.min(jnp.where(el == v1, lanef, big), axis=1, keepdims=True)
    el2 = jnp.where(lanef == i1, NEG_BIG, el)
    v2 = jnp.max(el2, axis=1, keepdims=True)
    i2 = jnp.min(jnp.where(el2 == v2, lanef, big), axis=1, keepdims=True)
    e21 = jnp.exp(v2 - v1)
    den = 1.0 + e21
    w1 = gw * (1.0 / den)
    w2 = gw * (e21 / den)
    return jnp.where(lane == 0, i1 - N_GROUPS,
                     jnp.where(lane == 1, i2 - N_GROUPS,
                               jnp.where(lane == 2, w1, jnp.where(lane == 3, w2, 0.0))))


def _outproj(yc, ya, x2, w_bf, g2, wr, br):
    s = x2.shape[0]
    tm = TM_PROJ
    const = lambda i: (0, 0)
    row = lambda i: (i, 0)
    return pl.pallas_call(
        _outproj_body,
        grid=(s // tm,),
        in_specs=[
            pl.BlockSpec((tm, D_CONV), row),
            pl.BlockSpec((tm, D_V), row),
            pl.BlockSpec((tm, D_MODEL), row),
            pl.BlockSpec((D_CONV + D_V, D_MODEL), const, pipeline_mode=pl.Buffered(1)),
            pl.BlockSpec((1, D_MODEL), const),
            pl.BlockSpec((D_MODEL, 2 * LANES), const),
            pl.BlockSpec((1, LANES), const),
        ],
        out_specs=[pl.BlockSpec((tm, D_MODEL), row), pl.BlockSpec((tm, LANES), row)],
        out_shape=[jax.ShapeDtypeStruct((s, D_MODEL), f32), jax.ShapeDtypeStruct((s, LANES), f32)],
        compiler_params=_cparams(1),
        name="outproj",
    )(yc, ya, x2, w_bf, g2, wr, br)


def _rank_body(r_ref, pos_ref, cnt_ref, run_sc, rank_sc):
    tb = TB_RANK
    ph = pl.program_id(0)
    i = pl.program_id(1)
    r = r_ref[...]
    lane = lax.broadcasted_iota(jnp.int32, r.shape, 1)
    lanef = lane.astype(f32)
    oh1 = (lanef == r[:, 0:1]).astype(f32)
    oh2 = (lanef == r[:, 1:2]).astype(f32)

    @pl.when((ph == 0) & (i == 0))
    def _():
        run_sc[...] = jnp.zeros(run_sc.shape, f32)

    @pl.when(ph == 0)
    def _():
        both = oh1 + oh2
        row = lax.broadcasted_iota(jnp.int32, (tb, tb), 0)
        col = lax.broadcasted_iota(jnp.int32, (tb, tb), 1)
        tri = (row > col).astype(bf16)
        before = jnp.dot(tri, both.astype(bf16), preferred_element_type=f32) + run_sc[...]
        rank1 = jnp.sum(before * oh1, axis=1, keepdims=True)
        rank2 = jnp.sum(before * oh2, axis=1, keepdims=True)
        rank_sc[i] = jnp.where(lane == 0, rank1, jnp.where(lane == 1, rank2, 0.0))
        run_sc[...] = run_sc[...] + jnp.sum(both, axis=0, keepdims=True)

    @pl.when(ph == 1)
    def _():
        cnt = run_sc[...]
        tiles = jnp.floor((cnt + (TM_EXP - 1)) * (1.0 / TM_EXP))
        lr = lax.broadcasted_iota(jnp.int32, (LANES, LANES), 0)
        lc = lax.broadcasted_iota(jnp.int32, (LANES, LANES), 1)
        upper = (lr < lc).astype(bf16)
        tiles8 = jnp.broadcast_to(tiles, (SUBLANES, LANES)).astype(bf16)
        offs = jnp.dot(tiles8, upper, preferred_element_type=f32)[0:1, :] * TM_EXP
        rk = rank_sc[i]
        pos1 = jnp.sum(oh1 * offs, axis=1, keepdims=True) + rk[:, 0:1]
        pos2 = jnp.sum(oh2 * offs, axis=1, keepdims=True) + rk[:, 1:2]
        pp = jnp.where(lane == 0, pos1, jnp.where(lane == 1, pos2, 0.0))
        pos_ref[...] = pp.T[0:SUBLANES, :].astype(jnp.int32)
        cnt_ref[...] = cnt


def _rank(r):
    s = r.shape[0]
    tb = TB_RANK
    nb = s // tb
    return pl.pallas_call(
        _rank_body,
        grid=(2, nb),
        in_specs=[pl.BlockSpec((tb, LANES), lambda ph, i: (i, 0))],
        out_specs=[pl.BlockSpec((SUBLANES, tb), lambda ph, i: (0, i * ph)),
                   pl.BlockSpec((1, LANES), lambda ph, i: (0, 0))],
        out_shape=[jax.ShapeDtypeStruct((SUBLANES, s), jnp.int32), jax.ShapeDtypeStruct((1, LANES), f32)],
        scratch_shapes=[pltpu.VMEM((1, LANES), f32), pltpu.VMEM((nb, tb, LANES), f32)],
        compiler_params=_cparams(2),
        name="rank",
    )(r)


def _row_copy(src, dst, sem):
    return pltpu.make_async_copy(src, dst, sem)


def _dispatch_body(ends_ref, nv_ref, pos1_ref, pos2_ref, h_ref, g2_ref, xs_ref, tbuf, zbuf, sems, zsem):
    tb, tm = TB_ROWS, TM_EXP
    i = pl.program_id(0)
    n = pl.num_programs(0)
    slot = i % 2

    def wait_slot(sl):
        for _ in range(2):
            _row_copy(tbuf.at[sl], xs_ref.at[pl.ds(0, tb)], sems.at[sl]).wait()

    @pl.when(i == 0)
    def _():
        zbuf[...] = jnp.zeros(zbuf.shape, f32)

        def zero_tile(t):
            return _row_copy(zbuf, xs_ref.at[pl.ds(pl.multiple_of(t * tm, tm), tm)], zsem)

        def each_fill(fn):
            for e in range(N_EXPERTS):
                first = 0 if e == 0 else ends_ref[e - 1]

                @pl.when(ends_ref[e] > first)
                def _():
                    fn(lax.shift_right_logical(ends_ref[e], tm.bit_length() - 1) - 1)

            def tail(t, carry):
                fn(t)
                return carry

            lax.fori_loop(nv_ref[0], xs_ref.shape[0] // tm, tail, 0)

        each_fill(lambda t: zero_tile(t).start())
        each_fill(lambda t: zero_tile(t).wait())

    @pl.when(i >= 2)
    def _():
        wait_slot(slot)

    h = h_ref[...]
    ms = jnp.mean(h * h, axis=-1, keepdims=True)
    tbuf[slot] = h * lax.rsqrt(ms + EPS) * g2_ref[...]

    for r in range(tb):
        src = tbuf.at[slot, pl.ds(r, 1)]
        _row_copy(src, xs_ref.at[pl.ds(pos1_ref[r], 1)], sems.at[slot]).start()
        _row_copy(src, xs_ref.at[pl.ds(pos2_ref[r], 1)], sems.at[slot]).start()

    @pl.when(i == n - 1)
    def _():
        wait_slot(slot)

        @pl.when(n >= 2)
        def _():
            wait_slot(1 - slot)


def _dispatch(ends, n_valid, pos1, pos2, h, g2, n_rows):
    s = h.shape[0]
    tb, tm = TB_ROWS, TM_EXP
    assert tm & (tm - 1) == 0 and n_rows % tm == 0
    grid_spec = pltpu.PrefetchScalarGridSpec(
        num_scalar_prefetch=2,
        grid=(s // tb,),
        in_specs=[
            pl.BlockSpec((tb,), lambda i, en, nv: (i,), memory_space=pltpu.SMEM),
            pl.BlockSpec((tb,), lambda i, en, nv: (i,), memory_space=pltpu.SMEM),
            pl.BlockSpec((tb, D_MODEL), lambda i, en, nv: (i, 0)),
            pl.BlockSpec((1, D_MODEL), lambda i, en, nv: (0, 0)),
        ],
        out_specs=pl.BlockSpec(memory_space=pl.ANY),
        scratch_shapes=[
            pltpu.VMEM((2, tb, D_MODEL), f32),
            pltpu.VMEM((tm, D_MODEL), f32),
            pltpu.SemaphoreType.DMA((2,)),
            pltpu.SemaphoreType.DMA,
        ],
    )
    return pl.pallas_call(
        _dispatch_body,
        grid_spec=grid_spec,
        out_shape=jax.ShapeDtypeStruct((n_rows, D_MODEL), f32),
        compiler_params=_cparams(1),
        name="dispatch",
    )(ends, n_valid, pos1, pos2, h, g2)


def _experts_body(te_ref, nv_ref, slot_ref, nxt_ref, xs_ref, wg_hbm, wu_hbm, wd_hbm, y_ref,
                  wg_f, wu_f, wd_f, wg_sc, wu_sc, wd_sc, wsem):
    i = pl.program_id(0)

    def weight_copies(e, sl):
        return [_row_copy(src.at[e], dst.at[sl], wsem.at[sl])
                for src, dst in ((wg_hbm, wg_f), (wu_hbm, wu_f), (wd_hbm, wd_f))]

    @pl.when(i < nv_ref[0])
    def _():
        e = te_ref[i]
        sl = slot_ref[i]
        prev = te_ref[jnp.maximum(i - 1, 0)]

        @pl.when(i == 0)
        def _():
            for c in weight_copies(e, sl):
                c.start()

        @pl.when((i == 0) | (e != prev))
        def _():
            for c in weight_copies(e, sl):
                c.wait()

            @pl.when(nxt_ref[i] >= 0)
            def _():
                for c in weight_copies(nxt_ref[i], 1 - sl):
                    c.start()

            wg_sc[...] = wg_f[sl].astype(bf16)
            wu_sc[...] = wu_f[sl].astype(bf16)
            wd_sc[...] = wd_f[sl].astype(bf16)

        x = xs_ref[...].astype(bf16)
        g = jnp.dot(x, wg_sc[...], preferred_element_type=f32)
        u = jnp.dot(x, wu_sc[...], preferred_element_type=f32)
        hm = (g * jax.nn.sigmoid(g) * u).astype(bf16)
        y_ref[...] = jnp.dot(hm, wd_sc[...], preferred_element_type=f32)

    @pl.when(i >= nv_ref[0])
    def _():
        y_ref[...] = jnp.zeros(y_ref.shape, f32)


def _experts(tile_expert, n_valid, tile_slot, tile_next, xs, w_gate, w_up, w_down):
    n_rows = xs.shape[0]
    tm = TM_EXP
    nt = n_rows // tm
    rows = lambda i, te, nv, ts, tn: (jnp.minimum(i, nv[0] - 1), 0)
    hbm = pl.BlockSpec(memory_space=pl.ANY)
    grid_spec = pltpu.PrefetchScalarGridSpec(
        num_scalar_prefetch=4,
        grid=(nt,),
        in_specs=[pl.BlockSpec((tm, D_MODEL), rows), hbm, hbm, hbm],
        out_specs=pl.BlockSpec((tm, D_MODEL), lambda i, te, nv, ts, tn: (i, 0)),
        scratch_shapes=[
            pltpu.VMEM((2, D_MODEL, D_EXPERT), f32),
            pltpu.VMEM((2, D_MODEL, D_EXPERT), f32),
            pltpu.VMEM((2, D_EXPERT, D_MODEL), f32),
            pltpu.VMEM((D_MODEL, D_EXPERT), bf16),
            pltpu.VMEM((D_MODEL, D_EXPERT), bf16),
            pltpu.VMEM((D_EXPERT, D_MODEL), bf16),
            pltpu.SemaphoreType.DMA((2,)),
        ],
    )
    return pl.pallas_call(
        _experts_body,
        grid_spec=grid_spec,
        out_shape=jax.ShapeDtypeStruct((n_rows, D_MODEL), f32),
        compiler_params=_cparams(1),
        name="experts",
    )(tile_expert, n_valid, tile_slot, tile_next, xs, w_gate, w_up, w_down)


def _combine_body(p1_ref, p2_ref, p1n_ref, p2n_ref, h_ref, r_ref, y_ref, o_ref, ybuf, sems):
    tb = TB_ROWS
    i = pl.program_id(0)
    n = pl.num_programs(0)
    slot = i % 2

    def issue_block(pa_ref, pb_ref, sl):
        for r in range(tb):
            _row_copy(y_ref.at[pl.ds(pa_ref[r], 1)], ybuf.at[sl, 0, pl.ds(r, 1)], sems.at[sl]).start()
            _row_copy(y_ref.at[pl.ds(pb_ref[r], 1)], ybuf.at[sl, 1, pl.ds(r, 1)], sems.at[sl]).start()

    @pl.when(i == 0)
    def _():
        issue_block(p1_ref, p2_ref, 0)

    @pl.when(i + 1 < n)
    def _():
        issue_block(p1n_ref, p2n_ref, 1 - slot)

    for k in range(2):
        _row_copy(y_ref.at[pl.ds(0, tb)], ybuf.at[slot, k], sems.at[slot]).wait()

    r = r_ref[...]
    o_ref[...] = h_ref[...] + r[:, 2:3] * ybuf[slot, 0] + r[:, 3:4] * ybuf[slot, 1]


def _combine(pos1, pos2, h, r, y):
    s = h.shape[0]
    tb = TB_ROWS
    nb = s // tb
    cur = lambda i: (i,)
    nxt = lambda i: (jnp.minimum(i + 1, nb - 1),)
    smem = functools.partial(pl.BlockSpec, (tb,), memory_space=pltpu.SMEM)
    return pl.pallas_call(
        _combine_body,
        grid=(nb,),
        in_specs=[
            smem(cur), smem(cur), smem(nxt), smem(nxt),
            pl.BlockSpec((tb, D_MODEL), lambda i: (i, 0)),
            pl.BlockSpec((tb, LANES), lambda i: (i, 0)),
            pl.BlockSpec(memory_space=pl.ANY),
        ],
        out_specs=pl.BlockSpec((tb, D_MODEL), lambda i: (i, 0)),
        out_shape=jax.ShapeDtypeStruct((s, D_MODEL), f32),
        scratch_shapes=[pltpu.VMEM((2, 2, tb, D_MODEL), f32), pltpu.SemaphoreType.DMA((2,))],
        compiler_params=_cparams(1),
        name="combine",
    )(pos1, pos2, pos1, pos2, h, r, y)


def kernel(x, norm1_g, w_in, conv_dw_kernel, conv_dw_bias, conv_ln_g, conv_ln_b, q_norm_g, k_norm_g,
           lambda_q1, lambda_k1, lambda_q2, lambda_k2, subln_g, w_out, norm2_g, w_group, b_group,
           w_router, b_router, w_gate, w_up, w_down):
    b, s, d = x.shape
    assert b == 1 and d == D_MODEL and norm1_g.shape[0] == 1
    x2 = x.reshape(s, d)

    w_in_bf = w_in[0].astype(bf16)
    w_out_bf = w_out[0].astype(bf16)
    qg = jnp.tile(q_norm_g[0].reshape(1, 2 * HEAD_DIM), (1, N_HEADS)) * (HEAD_DIM ** -0.5 * math.log2(math.e))
    kg = jnp.tile(k_norm_g[0].reshape(1, 2 * HEAD_DIM), (1, N_HEADS))
    blk = jnp.arange(256) // HEAD_DIM
    bd = jnp.where(blk[:, None] == blk[None, :], 1.0 / HEAD_DIM, 0.0).astype(bf16)
    n_r = N_GROUPS + N_EXPERTS
    wr = jnp.concatenate(
        [w_group[0], jnp.transpose(w_router[0], (1, 0, 2)).reshape(d, N_EXPERTS),
         jnp.zeros((d, LANES - n_r), f32)], axis=1)
    wr_hi = wr.astype(bf16)
    wr = jnp.concatenate([wr_hi, (wr - wr_hi.astype(f32)).astype(bf16)], axis=1)
    br = jnp.concatenate([b_group[0], b_router[0].reshape(N_EXPERTS), jnp.zeros((LANES - n_r,), f32)])[None]
    row = lambda a: a.reshape(1, -1)

    y_conv, qt, k, vt = _inproj(x2, norm1_g, w_in_bf, qg, kg, bd, conv_dw_kernel[0], row(conv_dw_bias[0]),
                                row(conv_ln_g[0]), row(conv_ln_b[0]))
    y_attn = _attn(qt, k, vt, row(lambda_q1[0]), row(lambda_k1[0]), row(lambda_q2[0]), row(lambda_k2[0]),
                   row(subln_g[0]))
    h, r = _outproj(y_conv, y_attn, x2, w_out_bf, norm2_g, wr, br)

    pos, cnt = _rank(r)
    pos1 = pos[0]
    pos2 = pos[1]
    tm = TM_EXP
    counts = cnt[0, :N_EXPERTS].astype(jnp.int32)
    padded = ((counts + tm - 1) // tm) * tm
    ends = jnp.cumsum(padded)
    n_rows = TOP_K * s + N_EXPERTS * tm
    nt = n_rows // tm
    n_valid = (ends[-1] // tm).astype(jnp.int32)
    tile_start = jnp.minimum(jnp.arange(nt, dtype=jnp.int32), n_valid - 1) * tm
    tile_expert = jnp.sum(tile_start[:, None] >= ends[None, :], axis=1).astype(jnp.int32)

    n_valid = n_valid.reshape(1)
    eidx = jnp.arange(N_EXPERTS, dtype=jnp.int32)
    nonempty = counts > 0
    slot_e = (jnp.cumsum(nonempty.astype(jnp.int32)) - 1) % 2
    later = jnp.where(nonempty[None, :] & (eidx[None, :] > eidx[:, None]), eidx[None, :], N_EXPERTS)
    next_e = jnp.min(later, axis=1)
    next_e = jnp.where(next_e == N_EXPERTS, -1, next_e).astype(jnp.int32)
    te_safe = jnp.minimum(tile_expert, N_EXPERTS - 1)
    tile_slot = slot_e[te_safe].astype(jnp.int32)
    tile_next = next_e[te_safe]

    xs = _dispatch(ends.astype(jnp.int32), n_valid, pos1, pos2, h, norm2_g, n_rows)
    ys = _experts(tile_expert, n_valid, tile_slot, tile_next, xs, w_gate[0], w_up[0], w_down[0])
    out = _combine(pos1, pos2, h, r, ys)
    return out.reshape(b, s, d)
```

```python
import functools
import math

import jax
import jax.numpy as jnp
from jax import lax
from jax.experimental import pallas as pl
from jax.experimental.pallas import tpu as pltpu

D_MODEL = 2048
D_CONV = 1024
N_HEADS = 8
HEAD_DIM = 64
V_DIM = 128
CONV_WIDTH = 31
CHUNK = 64
N_GROUPS = 4
EXPERTS_PER_GROUP = 8
N_EXPERTS = 32
TOP_K = 2
D_EXPERT = 512
D_Q = 1024
D_V = 1024
D_IN = 2 * D_CONV + 2 * D_Q + D_V
EPS = 1e-6
LAMBDA_INIT = 0.8 - 0.6 * math.exp(-0.3 * 0)

LANES = 128
SUBLANES = 8
VMEM_LIMIT = 56 * 1024 * 1024
NEG_BIG = -1e30

TM_PROJ = 512
TK_ATTN = 512
TQ_ATTN = 512
RC_CONV = 64
HALO = 32
TB_RANK = 512
TB_ROWS = 256
TM_EXP = 256

bf16 = jnp.bfloat16
f32 = jnp.float32


def _cparams(n_axes):
    return pltpu.CompilerParams(
        dimension_semantics=("arbitrary",) * n_axes, vmem_limit_bytes=VMEM_LIMIT)


def _inproj_body(x_ref, g1_ref, w_ref, qg_ref, kg_ref, bd_ref, cw_ref, cb_ref, lg_ref, lb_ref,
                 yc_ref, qt_ref, k_ref, vt_ref, pad_sc, acc_sc):
    i = pl.program_id(0)
    tm, rc = TM_PROJ, RC_CONV

    @pl.when(i == 0)
    def _():
        pad_sc[0:HALO, :] = jnp.zeros((HALO, D_CONV), f32)

    pad_sc[HALO + tm:, :] = jnp.zeros((SUBLANES, D_CONV), f32)

    x = x_ref[...]
    ms = jnp.mean(x * x, axis=-1, keepdims=True)
    xn = (x * lax.rsqrt(ms + EPS) * g1_ref[...]).astype(bf16)

    def mm(c0, n):
        return jnp.dot(xn, w_ref[:, c0:c0 + n], preferred_element_type=f32)

    nt = 512
    for c in range(0, D_CONV, nt):
        a = mm(c, nt)
        b = mm(D_CONV + c, nt)
        pad_sc[HALO:HALO + tm, c:c + nt] = a * jax.nn.sigmoid(b)

    bd = bd_ref[...]
    sl = 256

    def qk_norm(u, gain):
        ms64 = jnp.dot((u * u).astype(bf16), bd, preferred_element_type=f32)
        return u * lax.rsqrt(ms64 + EPS) * gain

    uq = mm(2 * D_CONV, D_Q)
    uk = mm(2 * D_CONV + D_Q, D_Q)
    uv = mm(2 * D_CONV + 2 * D_Q, D_V)
    for c in range(0, D_Q, sl):
        qn = qk_norm(uq[:, c:c + sl], qg_ref[:, c:c + sl])
        qt_ref[c:c + sl, :] = qn.T.astype(bf16)
    for c in range(0, D_Q, sl):
        k_ref[:, c:c + sl] = qk_norm(uk[:, c:c + sl], kg_ref[:, c:c + sl]).astype(bf16)
    for c in range(0, D_V, sl):
        vt_ref[0, c:c + sl, :] = uv[:, c:c + sl].T.astype(bf16)

    off = HALO - (CONV_WIDTH - 1)
    for c0 in range(0, D_CONV, LANES):
        cols = slice(c0, c0 + LANES)
        for r0 in range(0, tm, rc):
            acc = jnp.broadcast_to(cb_ref[:, cols], (rc, LANES))
            for a in range(SUBLANES):
                qa = None
                for jp in range(a, off + CONV_WIDTH, SUBLANES):
                    j = jp - off
                    if j < 0:
                        continue
                    base = r0 + jp - a
                    term = cw_ref[j:j + 1, cols] * pad_sc[base:base + rc + SUBLANES, cols]
                    qa = term if qa is None else qa + term
                acc = acc + qa[a:a + rc, :]
            acc_sc[r0:r0 + rc, cols] = acc
    pad_sc[0:HALO, :] = pad_sc[tm:tm + HALO, :]
    acc = acc_sc[...]
    mu = jnp.mean(acc, axis=-1, keepdims=True)
    xc = acc - mu
    var = jnp.mean(xc * xc, axis=-1, keepdims=True)
    y = xc * lax.rsqrt(var + EPS) * lg_ref[...] + lb_ref[...]
    yc_ref[...] = (y * jax.nn.sigmoid(y)).astype(bf16)


def _inproj(x2, g1, w_bf, qg, kg, bd, cw, cb, lg, lb):
    s = x2.shape[0]
    tm = TM_PROJ
    assert tm == TK_ATTN and s % tm == 0 and tm % RC_CONV == 0 and HALO >= CONV_WIDTH - 1
    const = lambda i: (0, 0)
    return pl.pallas_call(
        _inproj_body,
        grid=(s // tm,),
        in_specs=[
            pl.BlockSpec((tm, D_MODEL), lambda i: (i, 0)),
            pl.BlockSpec((1, D_MODEL), const),
            pl.BlockSpec((D_MODEL, D_IN), const, pipeline_mode=pl.Buffered(1)),
            pl.BlockSpec((1, D_Q), const),
            pl.BlockSpec((1, D_Q), const),
            pl.BlockSpec((256, 256), const),
            pl.BlockSpec((CONV_WIDTH, D_CONV), const),
            pl.BlockSpec((1, D_CONV), const),
            pl.BlockSpec((1, D_CONV), const),
            pl.BlockSpec((1, D_CONV), const),
        ],
        out_specs=[
            pl.BlockSpec((tm, D_CONV), lambda i: (i, 0)),
            pl.BlockSpec((D_Q, tm), lambda i: (0, i)),
            pl.BlockSpec((tm, D_Q), lambda i: (i, 0)),
            pl.BlockSpec((1, D_V, tm), lambda i: (i, 0, 0)),
        ],
        out_shape=[
            jax.ShapeDtypeStruct((s, D_CONV), bf16),
            jax.ShapeDtypeStruct((D_Q, s), bf16),
            jax.ShapeDtypeStruct((s, D_Q), bf16),
            jax.ShapeDtypeStruct((s // tm, D_V, tm), bf16),
        ],
        scratch_shapes=[pltpu.VMEM((HALO + tm + SUBLANES, D_CONV), f32), pltpu.VMEM((tm, D_CONV), f32)],
        compiler_params=_cparams(1),
        name="inproj",
    )(x2, g1, w_bf, qg, kg, bd, cw, cb, lg, lb)


def _attn_body(qt_ref, k_ref, vt_ref, lq1_ref, lk1_ref, lq2_ref, lk2_ref, sg_ref,
               o_ref, m_sc, l_sc, acc_sc, sa_sc, sb_sc):
    tq, tk = TQ_ATTN, TK_ATTN
    i = pl.program_id(1)
    qt = qt_ref[...]
    sub = lax.broadcasted_iota(jnp.int32, qt.shape, 0)
    zero = jnp.zeros_like(qt)
    qst = jnp.concatenate([jnp.where(sub < HEAD_DIM, qt, zero),
                           jnp.where(sub >= HEAD_DIM, qt, zero)], axis=1)

    m_sc[...] = jnp.full(m_sc.shape, NEG_BIG, f32)
    l_sc[...] = jnp.zeros(l_sc.shape, f32)
    acc_sc[...] = jnp.zeros(acc_sc.shape, f32)

    def scores(kb, dst):
        kblk = k_ref[pl.ds(pl.multiple_of(kb * tk, tk), tk), :]
        dst[...] = jnp.dot(kblk, qst, preferred_element_type=f32)

    def consume(src, kb, masked):
        s = src[...]
        if masked:
            key = lax.broadcasted_iota(jnp.int32, s.shape, 0)
            qry = lax.broadcasted_iota(jnp.int32, s.shape, 1)
            vis = (key // CHUNK) <= ((qry % tq) // CHUNK)
            s = jnp.where(vis, s, NEG_BIG)
        m_old = m_sc[...]
        m_new = jnp.maximum(m_old, jnp.max(s, axis=0, keepdims=True))
        alpha = jnp.exp2(m_old - m_new)
        p = jnp.exp2(s - m_new)
        l_sc[...] = alpha * l_sc[...] + jnp.sum(p, axis=0, keepdims=True)
        acc_sc[...] = alpha * acc_sc[...] + jnp.dot(vt_ref[kb], p.astype(bf16),
                                                    preferred_element_type=f32)
        m_sc[...] = m_new

    scores(i, sa_sc)
    scores(0, sb_sc)
    consume(sa_sc, i, True)

    def pair(j, carry):
        scores(2 * j + 1, sa_sc)
        consume(sb_sc, 2 * j, False)
        scores(2 * j + 2, sb_sc)
        consume(sa_sc, 2 * j + 1, False)
        return carry

    odd = i % 2 == 1
    lax.fori_loop(0, jnp.where(odd, i // 2, jnp.maximum(i // 2 - 1, 0)), pair, 0)

    @pl.when(odd)
    def _():
        consume(sb_sc, i - 1, False)

    @pl.when(jnp.logical_not(odd) & (i >= 2))
    def _():
        scores(i - 1, sa_sc)
        consume(sb_sc, i - 2, False)
        consume(sa_sc, i - 1, False)

    lam = (jnp.exp(jnp.sum(lq1_ref[...] * lk1_ref[...], axis=-1, keepdims=True))
           - jnp.exp(jnp.sum(lq2_ref[...] * lk2_ref[...], axis=-1, keepdims=True))
           + LAMBDA_INIT)
    o_all = acc_sc[...] / l_sc[...]
    o = (o_all[:, :tq] - lam * o_all[:, tq:]).T
    ms = jnp.mean(o * o, axis=-1, keepdims=True)
    y = o * lax.rsqrt(ms + EPS) * sg_ref[...] * (1.0 - LAMBDA_INIT)
    o_ref[...] = y.astype(bf16)


def _attn(qt, k, vt, lq1, lk1, lq2, lk2, sg):
    s = k.shape[0]
    tq, tk = TQ_ATTN, TK_ATTN
    assert tq == tk and s % tq == 0
    vec = lambda h, i: (0, 0)
    return pl.pallas_call(
        _attn_body,
        grid=(N_HEADS, s // tq),
        in_specs=[
            pl.BlockSpec((2 * HEAD_DIM, tq), lambda h, i: (h, i)),
            pl.BlockSpec((s, 2 * HEAD_DIM), lambda h, i: (0, h)),
            pl.BlockSpec((s // tk, V_DIM, tk), lambda h, i: (0, h, 0)),
            pl.BlockSpec((1, HEAD_DIM), vec),
            pl.BlockSpec((1, HEAD_DIM), vec),
            pl.BlockSpec((1, HEAD_DIM), vec),
            pl.BlockSpec((1, HEAD_DIM), vec),
            pl.BlockSpec((1, V_DIM), vec),
        ],
        out_specs=pl.BlockSpec((tq, V_DIM), lambda h, i: (i, h)),
        out_shape=jax.ShapeDtypeStruct((s, D_V), bf16),
        scratch_shapes=[
            pltpu.VMEM((1, 2 * tq), f32),
            pltpu.VMEM((1, 2 * tq), f32),
            pltpu.VMEM((V_DIM, 2 * tq), f32),
            pltpu.VMEM((tk, 2 * tq), f32),
            pltpu.VMEM((tk, 2 * tq), f32),
        ],
        compiler_params=_cparams(2),
        name="attn",
    )(qt, k, vt, lq1, lk1, lq2, lk2, sg)


def _outproj_body(yc_ref, ya_ref, x_ref, w_ref, g2_ref, wr_ref, br_ref, h_ref, r_ref):
    half = TM_PROJ // 2
    for r0 in range(0, TM_PROJ, half):
        rows = slice(r0, r0 + half)
        h = (jnp.dot(yc_ref[rows, :], w_ref[0:D_CONV, :], preferred_element_type=f32)
             + jnp.dot(ya_ref[rows, :], w_ref[D_CONV:, :], preferred_element_type=f32)
             + x_ref[rows, :])
        h_ref[rows, :] = h
        ms = jnp.mean(h * h, axis=-1, keepdims=True)
        t = h * lax.rsqrt(ms + EPS) * g2_ref[...]
        t_hi = t.astype(bf16)
        t_lo = (t - t_hi.astype(f32)).astype(bf16)
        hh = jnp.dot(t_hi, wr_ref[...], preferred_element_type=f32)
        lh = jnp.dot(t_lo, wr_ref[:, 0:LANES], preferred_element_type=f32)
        logits = hh[:, 0:LANES] + hh[:, LANES:] + lh + br_ref[...]
        r_ref[rows, :] = _route(logits)


def _route(logits):
    lane = lax.broadcasted_iota(jnp.int32, logits.shape, 1)
    lanef = lane.astype(f32)
    big = float(LANES)
    is_g = lane < N_GROUPS
    gl = jnp.where(is_g, logits, NEG_BIG)
    gmax = jnp.max(gl, axis=1, keepdims=True)
    gidx = jnp.min(jnp.where(gl == gmax, lanef, big), axis=1, keepdims=True)
    gsum = jnp.sum(jnp.where(is_g, jnp.exp(gl - gmax), 0.0), axis=1, keepdims=True)
    gw = 1.0 / gsum

    e_of_lane = lane - N_GROUPS
    in_group = ((e_of_lane >= 0) & (e_of_lane < N_EXPERTS)
                & ((e_of_lane // EXPERTS_PER_GROUP).astype(f32) == gidx))
    el = jnp.where(in_group, logits, NEG_BIG)
    v1 = jnp.max(el, axis=1, keepdims=True)
    i1 = jnp.min(jnp.where(el == v1, lanef, big), axis=1, keepdims=True)
    el2 = jnp.where(lanef == i1, NEG_BIG, el)
    v2 = jnp.max(el2, axis=1, keepdims=True)
    i2 = jnp.min(jnp.where(el2 == v2, lanef, big), axis=1, keepdims=True)
    e21 = jnp.exp(v2 - v1)
    den = 1.0 + e21
    w1 = gw * (1.0 / den)
    w2 = gw * (e21 / den)
    return jnp.where(lane == 0, i1 - N_GROUPS,
                     jnp.where(lane == 1, i2 - N_GROUPS,
                               jnp.where(lane == 2, w1, jnp.where(lane == 3, w2, 0.0))))


def _outproj(yc, ya, x2, w_bf, g2, wr, br):
    s = x2.shape[0]
    tm = TM_PROJ
    const = lambda i: (0, 0)
    row = lambda i: (i, 0)
    return pl.pallas_call(
        _outproj_body,
        grid=(s // tm,),
        in_specs=[
            pl.BlockSpec((tm, D_CONV), row),
            pl.BlockSpec((tm, D_V), row),
            pl.BlockSpec((tm, D_MODEL), row),
            pl.BlockSpec((D_CONV + D_V, D_MODEL), const, pipeline_mode=pl.Buffered(1)),
            pl.BlockSpec((1, D_MODEL), const),
            pl.BlockSpec((D_MODEL, 2 * LANES), const),
            pl.BlockSpec((1, LANES), const),
        ],
        out_specs=[pl.BlockSpec((tm, D_MODEL), row), pl.BlockSpec((tm, LANES), row)],
        out_shape=[jax.ShapeDtypeStruct((s, D_MODEL), f32), jax.ShapeDtypeStruct((s, LANES), f32)],
        compiler_params=_cparams(1),
        name="outproj",
    )(yc, ya, x2, w_bf, g2, wr, br)


def _rank_body(r_ref, pos_ref, cnt_ref, run_sc, rank_sc):
    tb = TB_RANK
    ph = pl.program_id(0)
    i = pl.program_id(1)
    r = r_ref[...]
    lane = lax.broadcasted_iota(jnp.int32, r.shape, 1)
    lanef = lane.astype(f32)
    oh1 = (lanef == r[:, 0:1]).astype(f32)
    oh2 = (lanef == r[:, 1:2]).astype(f32)

    @pl.when((ph == 0) & (i == 0))
    def _():
        run_sc[...] = jnp.zeros(run_sc.shape, f32)

    @pl.when(ph == 0)
    def _():
        both = oh1 + oh2
        row = lax.broadcasted_iota(jnp.int32, (tb, tb), 0)
        col = lax.broadcasted_iota(jnp.int32, (tb, tb), 1)
        tri = (row > col).astype(bf16)
        before = jnp.dot(tri, both.astype(bf16), preferred_element_type=f32) + run_sc[...]
        rank1 = jnp.sum(before * oh1, axis=1, keepdims=True)
        rank2 = jnp.sum(before * oh2, axis=1, keepdims=True)
        rank_sc[i] = jnp.where(lane == 0, rank1, jnp.where(lane == 1, rank2, 0.0))
        run_sc[...] = run_sc[...] + jnp.sum(both, axis=0, keepdims=True)

    @pl.when(ph == 1)
    def _():
        cnt = run_sc[...]
        tiles = jnp.floor((cnt + (TM_EXP - 1)) * (1.0 / TM_EXP))
        lr = lax.broadcasted_iota(jnp.int32, (LANES, LANES), 0)
        lc = lax.broadcasted_iota(jnp.int32, (LANES, LANES), 1)
        upper = (lr < lc).astype(bf16)
        tiles8 = jnp.broadcast_to(tiles, (SUBLANES, LANES)).astype(bf16)
        offs = jnp.dot(tiles8, upper, preferred_element_type=f32)[0:1, :] * TM_EXP
        rk = rank_sc[i]
        pos1 = jnp.sum(oh1 * offs, axis=1, keepdims=True) + rk[:, 0:1]
        pos2 = jnp.sum(oh2 * offs, axis=1, keepdims=True) + rk[:, 1:2]
        pp = jnp.where(lane == 0, pos1, jnp.where(lane == 1, pos2, 0.0))
        pos_ref[...] = pp.T[0:SUBLANES, :].astype(jnp.int32)
        cnt_ref[...] = cnt


def _rank(r):
    s = r.shape[0]
    tb = TB_RANK
    nb = s // tb
    return pl.pallas_call(
        _rank_body,
        grid=(2, nb),
        in_specs=[pl.BlockSpec((tb, LANES), lambda ph, i: (i, 0))],
        out_specs=[pl.BlockSpec((SUBLANES, tb), lambda ph, i: (0, i * ph)),
                   pl.BlockSpec((1, LANES), lambda ph, i: (0, 0))],
        out_shape=[jax.ShapeDtypeStruct((SUBLANES, s), jnp.int32), jax.ShapeDtypeStruct((1, LANES), f32)],
        scratch_shapes=[pltpu.VMEM((1, LANES), f32), pltpu.VMEM((nb, tb, LANES), f32)],
        compiler_params=_cparams(2),
        name="rank",
    )(r)


def _row_copy(src, dst, sem):
    return pltpu.make_async_copy(src, dst, sem)


def _dispatch_body(ends_ref, nv_ref, pos1_ref, pos2_ref, h_ref, g2_ref, xs_ref, tbuf, zbuf, sems, zsem):
    tb, tm = TB_ROWS, TM_EXP
    i = pl.program_id(0)
    n = pl.num_programs(0)
    slot = i % 2

    def wait_slot(sl):
        for _ in range(2):
            _row_copy(tbuf.at[sl], xs_ref.at[pl.ds(0, tb)], sems.at[sl]).wait()

    @pl.when(i == 0)
    def _():
        zbuf[...] = jnp.zeros(zbuf.shape, f32)

        def zero_tile(t):
            return _row_copy(zbuf, xs_ref.at[pl.ds(pl.multiple_of(t * tm, tm), tm)], zsem)

        def each_fill(fn):
            for e in range(N_EXPERTS):
                first = 0 if e == 0 else ends_ref[e - 1]

                @pl.when(ends_ref[e] > first)
                def _():
                    fn(lax.shift_right_logical(ends_ref[e], tm.bit_length() - 1) - 1)

            def tail(t, carry):
                fn(t)
                return carry

            lax.fori_loop(nv_ref[0], xs_ref.shape[0] // tm, tail, 0)

        each_fill(lambda t: zero_tile(t).start())
        each_fill(lambda t: zero_tile(t).wait())

    @pl.when(i >= 2)
    def _():
        wait_slot(slot)

    h = h_ref[...]
    ms = jnp.mean(h * h, axis=-1, keepdims=True)
    tbuf[slot] = h * lax.rsqrt(ms + EPS) * g2_ref[...]

    for r in range(tb):
        src = tbuf.at[slot, pl.ds(r, 1)]
        _row_copy(src, xs_ref.at[pl.ds(pos1_ref[r], 1)], sems.at[slot]).start()
        _row_copy(src, xs_ref.at[pl.ds(pos2_ref[r], 1)], sems.at[slot]).start()

    @pl.when(i == n - 1)
    def _():
        wait_slot(slot)

        @pl.when(n >= 2)
        def _():
            wait_slot(1 - slot)


def _dispatch(ends, n_valid, pos1, pos2, h, g2, n_rows):
    s = h.shape[0]
    tb, tm = TB_ROWS, TM_EXP
    assert tm & (tm - 1) == 0 and n_rows % tm == 0
    grid_spec = pltpu.PrefetchScalarGridSpec(
        num_scalar_prefetch=2,
        grid=(s // tb,),
        in_specs=[
            pl.BlockSpec((tb,), lambda i, en, nv: (i,), memory_space=pltpu.SMEM),
            pl.BlockSpec((tb,), lambda i, en, nv: (i,), memory_space=pltpu.SMEM),
            pl.BlockSpec((tb, D_MODEL), lambda i, en, nv: (i, 0)),
            pl.BlockSpec((1, D_MODEL), lambda i, en, nv: (0, 0)),
        ],
        out_specs=pl.BlockSpec(memory_space=pl.ANY),
        scratch_shapes=[
            pltpu.VMEM((2, tb, D_MODEL), f32),
            pltpu.VMEM((tm, D_MODEL), f32),
            pltpu.SemaphoreType.DMA((2,)),
            pltpu.SemaphoreType.DMA,
        ],
    )
    return pl.pallas_call(
        _dispatch_body,
        grid_spec=grid_spec,
        out_shape=jax.ShapeDtypeStruct((n_rows, D_MODEL), f32),
        compiler_params=_cparams(1),
        name="dispatch",
    )(ends, n_valid, pos1, pos2, h, g2)


def _experts_body(te_ref, nv_ref, slot_ref, nxt_ref, xs_ref, wg_hbm, wu_hbm, wd_hbm, y_ref,
                  wg_f, wu_f, wd_f, wg_sc, wu_sc, wd_sc, wsem):
    i = pl.program_id(0)

    def weight_copies(e, sl):
        return [_row_copy(src.at[e], dst.at[sl], wsem.at[sl])
                for src, dst in ((wg_hbm, wg_f), (wu_hbm, wu_f), (wd_hbm, wd_f))]

    @pl.when(i < nv_ref[0])
    def _():
        e = te_ref[i]
        sl = slot_ref[i]
        prev = te_ref[jnp.maximum(i - 1, 0)]

        @pl.when(i == 0)
        def _():
            for c in weight_copies(e, sl):
                c.start()

        @pl.when((i == 0) | (e != prev))
        def _():
            for c in weight_copies(e, sl):
                c.wait()

            @pl.when(nxt_ref[i] >= 0)
            def _():
                for c in weight_copies(nxt_ref[i], 1 - sl):
                    c.start()

            wg_sc[...] = wg_f[sl].astype(bf16)
            wu_sc[...] = wu_f[sl].astype(bf16)
            wd_sc[...] = wd_f[sl].astype(bf16)

        x = xs_ref[...].astype(bf16)
        g = jnp.dot(x, wg_sc[...], preferred_element_type=f32)
        u = jnp.dot(x, wu_sc[...], preferred_element_type=f32)
        hm = (g * jax.nn.sigmoid(g) * u).astype(bf16)
        y_ref[...] = jnp.dot(hm, wd_sc[...], preferred_element_type=f32)

    @pl.when(i >= nv_ref[0])
    def _():
        y_ref[...] = jnp.zeros(y_ref.shape, f32)


def _experts(tile_expert, n_valid, tile_slot, tile_next, xs, w_gate, w_up, w_down):
    n_rows = xs.shape[0]
    tm = TM_EXP
    nt = n_rows // tm
    rows = lambda i, te, nv, ts, tn: (jnp.minimum(i, nv[0] - 1), 0)
    hbm = pl.BlockSpec(memory_space=pl.ANY)
    grid_spec = pltpu.PrefetchScalarGridSpec(
        num_scalar_prefetch=4,
        grid=(nt,),
        in_specs=[pl.BlockSpec((tm, D_MODEL), rows), hbm, hbm, hbm],
        out_specs=pl.BlockSpec((tm, D_MODEL), lambda i, te, nv, ts, tn: (i, 0)),
        scratch_shapes=[
            pltpu.VMEM((2, D_MODEL, D_EXPERT), f32),
            pltpu.VMEM((2, D_MODEL, D_EXPERT), f32),
            pltpu.VMEM((2, D_EXPERT, D_MODEL), f32),
            pltpu.VMEM((D_MODEL, D_EXPERT), bf16),
            pltpu.VMEM((D_MODEL, D_EXPERT), bf16),
            pltpu.VMEM((D_EXPERT, D_MODEL), bf16),
            pltpu.SemaphoreType.DMA((2,)),
        ],
    )
    return pl.pallas_call(
        _experts_body,
        grid_spec=grid_spec,
        out_shape=jax.ShapeDtypeStruct((n_rows, D_MODEL), f32),
        compiler_params=_cparams(1),
        name="experts",
    )(tile_expert, n_valid, tile_slot, tile_next, xs, w_gate, w_up, w_down)


def _combine_body(p1_ref, p2_ref, p1n_ref, p2n_ref, h_ref, r_ref, y_ref, o_ref, ybuf, sems):
    tb = TB_ROWS
    i = pl.program_id(0)
    n = pl.num_programs(0)
    slot = i % 2

    def issue_block(pa_ref, pb_ref, sl):
        for r in range(tb):
            _row_copy(y_ref.at[pl.ds(pa_ref[r], 1)], ybuf.at[sl, 0, pl.ds(r, 1)], sems.at[sl]).start()
            _row_copy(y_ref.at[pl.ds(pb_ref[r], 1)], ybuf.at[sl, 1, pl.ds(r, 1)], sems.at[sl]).start()

    @pl.when(i == 0)
    def _():
        issue_block(p1_ref, p2_ref, 0)

    @pl.when(i + 1 < n)
    def _():
        issue_block(p1n_ref, p2n_ref, 1 - slot)

    for k in range(2):
        _row_copy(y_ref.at[pl.ds(0, tb)], ybuf.at[slot, k], sems.at[slot]).wait()

    r = r_ref[...]
    o_ref[...] = h_ref[...] + r[:, 2:3] * ybuf[slot, 0] + r[:, 3:4] * ybuf[slot, 1]


def _combine(pos1, pos2, h, r, y):
    s = h.shape[0]
    tb = TB_ROWS
    nb = s // tb
    cur = lambda i: (i,)
    nxt = lambda i: (jnp.minimum(i + 1, nb - 1),)
    smem = functools.partial(pl.BlockSpec, (tb,), memory_space=pltpu.SMEM)
    return pl.pallas_call(
        _combine_body,
        grid=(nb,),
        in_specs=[
            smem(cur), smem(cur), smem(nxt), smem(nxt),
            pl.BlockSpec((tb, D_MODEL), lambda i: (i, 0)),
            pl.BlockSpec((tb, LANES), lambda i: (i, 0)),
            pl.BlockSpec(memory_space=pl.ANY),
        ],
        out_specs=pl.BlockSpec((tb, D_MODEL), lambda i: (i, 0)),
        out_shape=jax.ShapeDtypeStruct((s, D_MODEL), f32),
        scratch_shapes=[pltpu.VMEM((2, 2, tb, D_MODEL), f32), pltpu.SemaphoreType.DMA((2,))],
        compiler_params=_cparams(1),
        name="combine",
    )(pos1, pos2, pos1, pos2, h, r, y)


def kernel(x, norm1_g, w_in, conv_dw_kernel, conv_dw_bias, conv_ln_g, conv_ln_b, q_norm_g, k_norm_g,
           lambda_q1, lambda_k1, lambda_q2, lambda_k2, subln_g, w_out, norm2_g, w_group, b_group,
           w_router, b_router, w_gate, w_up, w_down):
    b, s, d = x.shape
    assert b == 1 and d == D_MODEL and norm1_g.shape[0] == 1
    x2 = x.reshape(s, d)

    w_in_bf = w_in[0].astype(bf16)
    w_out_bf = w_out[0].astype(bf16)
    qg = jnp.tile(q_norm_g[0].reshape(1, 2 * HEAD_DIM), (1, N_HEADS)) * (HEAD_DIM ** -0.5 * math.log2(math.e))
    kg = jnp.tile(k_norm_g[0].reshape(1, 2 * HEAD_DIM), (1, N_HEADS))
    blk = jnp.arange(256) // HEAD_DIM
    bd = jnp.where(blk[:, None] == blk[None, :], 1.0 / HEAD_DIM, 0.0).astype(bf16)
    n_r = N_GROUPS + N_EXPERTS
    wr = jnp.concatenate(
        [w_group[0], jnp.transpose(w_router[0], (1, 0, 2)).reshape(d, N_EXPERTS),
         jnp.zeros((d, LANES - n_r), f32)], axis=1)
    wr_hi = wr.astype(bf16)
    wr = jnp.concatenate([wr_hi, (wr - wr_hi.astype(f32)).astype(bf16)], axis=1)
    br = jnp.concatenate([b_group[0], b_router[0].reshape(N_EXPERTS), jnp.zeros((LANES - n_r,), f32)])[None]
    row = lambda a: a.reshape(1, -1)

    y_conv, qt, k, vt = _inproj(x2, norm1_g, w_in_bf, qg, kg, bd, conv_dw_kernel[0], row(conv_dw_bias[0]),
                                row(conv_ln_g[0]), row(conv_ln_b[0]))
    y_attn = _attn(qt, k, vt, row(lambda_q1[0]), row(lambda_k1[0]), row(lambda_q2[0]), row(lambda_k2[0]),
                   row(subln_g[0]))
    h, r = _outproj(y_conv, y_attn, x2, w_out_bf, norm2_g, wr, br)

    pos, cnt = _rank(r)
    pos1 = pos[0]
    pos2 = pos[1]
    tm = TM_EXP
    counts = cnt[0, :N_EXPERTS].astype(jnp.int32)
    padded = ((counts + tm - 1) // tm) * tm
    ends = jnp.cumsum(padded)
    n_rows = TOP_K * s + N_EXPERTS * tm
    nt = n_rows // tm
    n_valid = (ends[-1] // tm).astype(jnp.int32)
    tile_start = jnp.minimum(jnp.arange(nt, dtype=jnp.int32), n_valid - 1) * tm
    tile_expert = jnp.sum(tile_start[:, None] >= ends[None, :], axis=1).astype(jnp.int32)

    n_valid = n_valid.reshape(1)
    eidx = jnp.arange(N_EXPERTS, dtype=jnp.int32)
    nonempty = counts > 0
    slot_e = (jnp.cumsum(nonempty.astype(jnp.int32)) - 1) % 2
    later = jnp.where(nonempty[None, :] & (eidx[None, :] > eidx[:, None]), eidx[None, :], N_EXPERTS)
    next_e = jnp.min(later, axis=1)
    next_e = jnp.where(next_e == N_EXPERTS, -1, next_e).astype(jnp.int32)
    is_e = tile_expert[:, None] == eidx[None, :]
    tile_slot = jnp.sum(jnp.where(is_e, slot_e[None, :], 0), axis=1).astype(jnp.int32)
    tile_next = jnp.sum(jnp.where(is_e, next_e[None, :], 0), axis=1).astype(jnp.int32)

    xs = _dispatch(ends.astype(jnp.int32), n_valid, pos1, pos2, h, norm2_g, n_rows)
    ys = _experts(tile_expert, n_valid, tile_slot, tile_next, xs, w_gate[0], w_up[0], w_down[0])
    out = _combine(pos1, pos2, h, r, ys)
    return out.reshape(b, s, d)
```

```python
import functools
import math

import jax
import jax.numpy as jnp
from jax import lax
from jax.experimental import pallas as pl
from jax.experimental.pallas import tpu as pltpu

D_MODEL = 2048
D_CONV = 1024
N_HEADS = 8
HEAD_DIM = 64
V_DIM = 128
CONV_WIDTH = 31
CHUNK = 64
N_GROUPS = 4
EXPERTS_PER_GROUP = 8
N_EXPERTS = 32
TOP_K = 2
D_EXPERT = 512
D_Q = 1024
D_V = 1024
D_IN = 2 * D_CONV + 2 * D_Q + D_V
EPS = 1e-6
LAMBDA_INIT = 0.8 - 0.6 * math.exp(-0.3 * 0)

LANES = 128
SUBLANES = 8
VMEM_LIMIT = 56 * 1024 * 1024
NEG_BIG = -1e30

TM_PROJ = 512
TK_ATTN = 512
TQ_ATTN = 512
RC_CONV = 64
HALO = 32
TB_RANK = 512
TB_ROWS = 256
TM_EXP = 256

bf16 = jnp.bfloat16
f32 = jnp.float32


def _cparams(n_axes):
    return pltpu.CompilerParams(
        dimension_semantics=("arbitrary",) * n_axes, vmem_limit_bytes=VMEM_LIMIT)


def _inproj_body(x_ref, g1_ref, w_ref, qg_ref, kg_ref, bd_ref, cw_ref, cb_ref, lg_ref, lb_ref,
                 yc_ref, qt_ref, k_ref, vt_ref, pad_sc, acc_sc):
    i = pl.program_id(0)
    tm, rc = TM_PROJ, RC_CONV

    @pl.when(i == 0)
    def _():
        pad_sc[0:HALO, :] = jnp.zeros((HALO, D_CONV), f32)

    pad_sc[HALO + tm:, :] = jnp.zeros((SUBLANES, D_CONV), f32)

    x = x_ref[...]
    ms = jnp.mean(x * x, axis=-1, keepdims=True)
    xn = (x * lax.rsqrt(ms + EPS) * g1_ref[...]).astype(bf16)

    def mm(c0, n):
        return jnp.dot(xn, w_ref[:, c0:c0 + n], preferred_element_type=f32)

    nt = 512
    for c in range(0, D_CONV, nt):
        a = mm(c, nt)
        b = mm(D_CONV + c, nt)
        pad_sc[HALO:HALO + tm, c:c + nt] = a * jax.nn.sigmoid(b)

    bd = bd_ref[...]
    sl = 256

    def qk_norm(u, gain):
        ms64 = jnp.dot((u * u).astype(bf16), bd, preferred_element_type=f32)
        return u * lax.rsqrt(ms64 + EPS) * gain

    uq = mm(2 * D_CONV, D_Q)
    uk = mm(2 * D_CONV + D_Q, D_Q)
    uv = mm(2 * D_CONV + 2 * D_Q, D_V)
    for c in range(0, D_Q, sl):
        qn = qk_norm(uq[:, c:c + sl], qg_ref[:, c:c + sl])
        qt_ref[c:c + sl, :] = qn.T.astype(bf16)
    for c in range(0, D_Q, sl):
        k_ref[:, c:c + sl] = qk_norm(uk[:, c:c + sl], kg_ref[:, c:c + sl]).astype(bf16)
    for c in range(0, D_V, sl):
        vt_ref[0, c:c + sl, :] = uv[:, c:c + sl].T.astype(bf16)

    off = HALO - (CONV_WIDTH - 1)
    for c0 in range(0, D_CONV, LANES):
        cols = slice(c0, c0 + LANES)
        for r0 in range(0, tm, rc):
            acc = jnp.broadcast_to(cb_ref[:, cols], (rc, LANES))
            for a in range(SUBLANES):
                qa = None
                for jp in range(a, off + CONV_WIDTH, SUBLANES):
                    j = jp - off
                    if j < 0:
                        continue
                    base = r0 + jp - a
                    term = cw_ref[j:j + 1, cols] * pad_sc[base:base + rc + SUBLANES, cols]
                    qa = term if qa is None else qa + term
                acc = acc + qa[a:a + rc, :]
            acc_sc[r0:r0 + rc, cols] = acc
    pad_sc[0:HALO, :] = pad_sc[tm:tm + HALO, :]
    acc = acc_sc[...]
    mu = jnp.mean(acc, axis=-1, keepdims=True)
    xc = acc - mu
    var = jnp.mean(xc * xc, axis=-1, keepdims=True)
    y = xc * lax.rsqrt(var + EPS) * lg_ref[...] + lb_ref[...]
    yc_ref[...] = (y * jax.nn.sigmoid(y)).astype(bf16)


def _inproj(x2, g1, w_bf, qg, kg, bd, cw, cb, lg, lb):
    s = x2.shape[0]
    tm = TM_PROJ
    assert tm == TK_ATTN and s % tm == 0 and tm % RC_CONV == 0 and HALO >= CONV_WIDTH - 1
    const = lambda i: (0, 0)
    return pl.pallas_call(
        _inproj_body,
        grid=(s // tm,),
        in_specs=[
            pl.BlockSpec((tm, D_MODEL), lambda i: (i, 0)),
            pl.BlockSpec((1, D_MODEL), const),
            pl.BlockSpec((D_MODEL, D_IN), const, pipeline_mode=pl.Buffered(1)),
            pl.BlockSpec((1, D_Q), const),
            pl.BlockSpec((1, D_Q), const),
            pl.BlockSpec((256, 256), const),
            pl.BlockSpec((CONV_WIDTH, D_CONV), const),
            pl.BlockSpec((1, D_CONV), const),
            pl.BlockSpec((1, D_CONV), const),
            pl.BlockSpec((1, D_CONV), const),
        ],
        out_specs=[
            pl.BlockSpec((tm, D_CONV), lambda i: (i, 0)),
            pl.BlockSpec((D_Q, tm), lambda i: (0, i)),
            pl.BlockSpec((tm, D_Q), lambda i: (i, 0)),
            pl.BlockSpec((1, D_V, tm), lambda i: (i, 0, 0)),
        ],
        out_shape=[
            jax.ShapeDtypeStruct((s, D_CONV), bf16),
            jax.ShapeDtypeStruct((D_Q, s), bf16),
            jax.ShapeDtypeStruct((s, D_Q), bf16),
            jax.ShapeDtypeStruct((s // tm, D_V, tm), bf16),
        ],
        scratch_shapes=[pltpu.VMEM((HALO + tm + SUBLANES, D_CONV), f32), pltpu.VMEM((tm, D_CONV), f32)],
        compiler_params=_cparams(1),
        name="inproj",
    )(x2, g1, w_bf, qg, kg, bd, cw, cb, lg, lb)


def _attn_body(qt_ref, k_ref, vt_ref, lq1_ref, lk1_ref, lq2_ref, lk2_ref, sg_ref,
               o_ref, m_sc, l_sc, acc_sc, sa_sc, sb_sc):
    tq, tk = TQ_ATTN, TK_ATTN
    i = pl.program_id(1)
    qt = qt_ref[...]
    sub = lax.broadcasted_iota(jnp.int32, qt.shape, 0)
    zero = jnp.zeros_like(qt)
    qst = jnp.concatenate([jnp.where(sub < HEAD_DIM, qt, zero),
                           jnp.where(sub >= HEAD_DIM, qt, zero)], axis=1)

    m_sc[...] = jnp.full(m_sc.shape, NEG_BIG, f32)
    l_sc[...] = jnp.zeros(l_sc.shape, f32)
    acc_sc[...] = jnp.zeros(acc_sc.shape, f32)

    def scores(kb, dst):
        kblk = k_ref[pl.ds(pl.multiple_of(kb * tk, tk), tk), :]
        dst[...] = jnp.dot(kblk, qst, preferred_element_type=f32)

    def consume(src, kb, masked):
        s = src[...]
        if masked:
            key = lax.broadcasted_iota(jnp.int32, s.shape, 0)
            qry = lax.broadcasted_iota(jnp.int32, s.shape, 1)
            vis = (key // CHUNK) <= ((qry % tq) // CHUNK)
            s = jnp.where(vis, s, NEG_BIG)
        m_old = m_sc[...]
        m_new = jnp.maximum(m_old, jnp.max(s, axis=0, keepdims=True))
        alpha = jnp.exp2(m_old - m_new)
        p = jnp.exp2(s - m_new)
        l_sc[...] = alpha * l_sc[...] + jnp.sum(p, axis=0, keepdims=True)
        acc_sc[...] = alpha * acc_sc[...] + jnp.dot(vt_ref[kb], p.astype(bf16),
                                                    preferred_element_type=f32)
        m_sc[...] = m_new

    last_full = jnp.maximum(i - 1, 0)
    scores(i, sa_sc)
    scores(0, sb_sc)
    consume(sa_sc, i, True)

    def pair(j, carry):
        scores(2 * j + 1, sa_sc)
        consume(sb_sc, 2 * j, False)
        scores(jnp.minimum(2 * j + 2, last_full), sb_sc)
        consume(sa_sc, 2 * j + 1, False)
        return carry

    lax.fori_loop(0, i // 2, pair, 0)

    @pl.when(i % 2 == 1)
    def _():
        consume(sb_sc, i - 1, False)

    lam = (jnp.exp(jnp.sum(lq1_ref[...] * lk1_ref[...], axis=-1, keepdims=True))
           - jnp.exp(jnp.sum(lq2_ref[...] * lk2_ref[...], axis=-1, keepdims=True))
           + LAMBDA_INIT)
    o_all = acc_sc[...] / l_sc[...]
    o = (o_all[:, :tq] - lam * o_all[:, tq:]).T
    ms = jnp.mean(o * o, axis=-1, keepdims=True)
    y = o * lax.rsqrt(ms + EPS) * sg_ref[...] * (1.0 - LAMBDA_INIT)
    o_ref[...] = y.astype(bf16)


def _attn(qt, k, vt, lq1, lk1, lq2, lk2, sg):
    s = k.shape[0]
    tq, tk = TQ_ATTN, TK_ATTN
    assert tq == tk and s % tq == 0
    vec = lambda h, i: (0, 0)
    return pl.pallas_call(
        _attn_body,
        grid=(N_HEADS, s // tq),
        in_specs=[
            pl.BlockSpec((2 * HEAD_DIM, tq), lambda h, i: (h, i)),
            pl.BlockSpec((s, 2 * HEAD_DIM), lambda h, i: (0, h)),
            pl.BlockSpec((s // tk, V_DIM, tk), lambda h, i: (0, h, 0)),
            pl.BlockSpec((1, HEAD_DIM), vec),
            pl.BlockSpec((1, HEAD_DIM), vec),
            pl.BlockSpec((1, HEAD_DIM), vec),
            pl.BlockSpec((1, HEAD_DIM), vec),
            pl.BlockSpec((1, V_DIM), vec),
        ],
        out_specs=pl.BlockSpec((tq, V_DIM), lambda h, i: (i, h)),
        out_shape=jax.ShapeDtypeStruct((s, D_V), bf16),
        scratch_shapes=[
            pltpu.VMEM((1, 2 * tq), f32),
            pltpu.VMEM((1, 2 * tq), f32),
            pltpu.VMEM((V_DIM, 2 * tq), f32),
            pltpu.VMEM((tk, 2 * tq), f32),
            pltpu.VMEM((tk, 2 * tq), f32),
        ],
        compiler_params=_cparams(2),
        name="attn",
    )(qt, k, vt, lq1, lk1, lq2, lk2, sg)


def _outproj_body(yc_ref, ya_ref, x_ref, w_ref, g2_ref, wr_ref, br_ref, h_ref, r_ref):
    half = TM_PROJ // 2
    for r0 in range(0, TM_PROJ, half):
        rows = slice(r0, r0 + half)
        h = (jnp.dot(yc_ref[rows, :], w_ref[0:D_CONV, :], preferred_element_type=f32)
             + jnp.dot(ya_ref[rows, :], w_ref[D_CONV:, :], preferred_element_type=f32)
             + x_ref[rows, :])
        h_ref[rows, :] = h
        ms = jnp.mean(h * h, axis=-1, keepdims=True)
        t = h * lax.rsqrt(ms + EPS) * g2_ref[...]
        t_hi = t.astype(bf16)
        t_lo = (t - t_hi.astype(f32)).astype(bf16)
        hh = jnp.dot(t_hi, wr_ref[...], preferred_element_type=f32)
        lh = jnp.dot(t_lo, wr_ref[:, 0:LANES], preferred_element_type=f32)
        logits = hh[:, 0:LANES] + hh[:, LANES:] + lh + br_ref[...]
        r_ref[rows, :] = _route(logits)


def _route(logits):
    lane = lax.broadcasted_iota(jnp.int32, logits.shape, 1)
    lanef = lane.astype(f32)
    big = float(LANES)
    is_g = lane < N_GROUPS
    gl = jnp.where(is_g, logits, NEG_BIG)
    gmax = jnp.max(gl, axis=1, keepdims=True)
    gidx = jnp.min(jnp.where(gl == gmax, lanef, big), axis=1, keepdims=True)
    gsum = jnp.sum(jnp.where(is_g, jnp.exp(gl - gmax), 0.0), axis=1, keepdims=True)
    gw = 1.0 / gsum

    e_of_lane = lane - N_GROUPS
    in_group = ((e_of_lane >= 0) & (e_of_lane < N_EXPERTS)
                & ((e_of_lane // EXPERTS_PER_GROUP).astype(f32) == gidx))
    el = jnp.where(in_group, logits, NEG_BIG)
    v1 = jnp.max(el, axis=1, keepdims=True)
    i1 = jnp.min(jnp.where(el == v1, lanef, big), axis=1, keepdims=True)
    el2 = jnp.where(lanef == i1, NEG_BIG, el)
    v2 = jnp.max(el2, axis=1, keepdims=True)
    i2 = jnp.min(jnp.where(el2 == v2, lanef, big), axis=1, keepdims=True)
    e21 = jnp.exp(v2 - v1)
    den = 1.0 + e21
    w1 = gw * (1.0 / den)
    w2 = gw * (e21 / den)
    return jnp.where(lane == 0, i1 - N_GROUPS,
                     jnp.where(lane == 1, i2 - N_GROUPS,
                               jnp.where(lane == 2, w1, jnp.where(lane == 3, w2, 0.0))))


def _outproj(yc, ya, x2, w_bf, g2, wr, br):
    s = x2.shape[0]
    tm = TM_PROJ
    const = lambda i: (0, 0)
    row = lambda i: (i, 0)
    return pl.pallas_call(
        _outproj_body,
        grid=(s // tm,),
        in_specs=[
            pl.BlockSpec((tm, D_CONV), row),
            pl.BlockSpec((tm, D_V), row),
            pl.BlockSpec((tm, D_MODEL), row),
            pl.BlockSpec((D_CONV + D_V, D_MODEL), const, pipeline_mode=pl.Buffered(1)),
            pl.BlockSpec((1, D_MODEL), const),
            pl.BlockSpec((D_MODEL, 2 * LANES), const),
            pl.BlockSpec((1, LANES), const),
        ],
        out_specs=[pl.BlockSpec((tm, D_MODEL), row), pl.BlockSpec((tm, LANES), row)],
        out_shape=[jax.ShapeDtypeStruct((s, D_MODEL), f32), jax.ShapeDtypeStruct((s, LANES), f32)],
        compiler_params=_cparams(1),
        name="outproj",
    )(yc, ya, x2, w_bf, g2, wr, br)


def _rank_body(r_ref, pos_ref, cnt_ref, run_sc, rank_sc):
    tb = TB_RANK
    ph = pl.program_id(0)
    i = pl.program_id(1)
    r = r_ref[...]
    lane = lax.broadcasted_iota(jnp.int32, r.shape, 1)
    lanef = lane.astype(f32)
    oh1 = (lanef == r[:, 0:1]).astype(f32)
    oh2 = (lanef == r[:, 1:2]).astype(f32)

    @pl.when((ph == 0) & (i == 0))
    def _():
        run_sc[...] = jnp.zeros(run_sc.shape, f32)

    @pl.when(ph == 0)
    def _():
        both = oh1 + oh2
        row = lax.broadcasted_iota(jnp.int32, (tb, tb), 0)
        col = lax.broadcasted_iota(jnp.int32, (tb, tb), 1)
        tri = (row > col).astype(bf16)
        before = jnp.dot(tri, both.astype(bf16), preferred_element_type=f32) + run_sc[...]
        rank1 = jnp.sum(before * oh1, axis=1, keepdims=True)
        rank2 = jnp.sum(before * oh2, axis=1, keepdims=True)
        rank_sc[i] = jnp.where(lane == 0, rank1, jnp.where(lane == 1, rank2, 0.0))
        run_sc[...] = run_sc[...] + jnp.sum(both, axis=0, keepdims=True)

    @pl.when(ph == 1)
    def _():
        cnt = run_sc[...]
        tiles = jnp.floor((cnt + (TM_EXP - 1)) * (1.0 / TM_EXP))
        lr = lax.broadcasted_iota(jnp.int32, (LANES, LANES), 0)
        lc = lax.broadcasted_iota(jnp.int32, (LANES, LANES), 1)
        upper = (lr < lc).astype(bf16)
        tiles8 = jnp.broadcast_to(tiles, (SUBLANES, LANES)).astype(bf16)
        offs = jnp.dot(tiles8, upper, preferred_element_type=f32)[0:1, :] * TM_EXP
        rk = rank_sc[i]
        pos1 = jnp.sum(oh1 * offs, axis=1, keepdims=True) + rk[:, 0:1]
        pos2 = jnp.sum(oh2 * offs, axis=1, keepdims=True) + rk[:, 1:2]
        pp = jnp.where(lane == 0, pos1, jnp.where(lane == 1, pos2, 0.0))
        pos_ref[...] = pp.T[0:SUBLANES, :].astype(jnp.int32)
        cnt_ref[...] = cnt


def _rank(r):
    s = r.shape[0]
    tb = TB_RANK
    nb = s // tb
    return pl.pallas_call(
        _rank_body,
        grid=(2, nb),
        in_specs=[pl.BlockSpec((tb, LANES), lambda ph, i: (i, 0))],
        out_specs=[pl.BlockSpec((SUBLANES, tb), lambda ph, i: (0, i * ph)),
                   pl.BlockSpec((1, LANES), lambda ph, i: (0, 0))],
        out_shape=[jax.ShapeDtypeStruct((SUBLANES, s), jnp.int32), jax.ShapeDtypeStruct((1, LANES), f32)],
        scratch_shapes=[pltpu.VMEM((1, LANES), f32), pltpu.VMEM((nb, tb, LANES), f32)],
        compiler_params=_cparams(2),
        name="rank",
    )(r)


def _row_copy(src, dst, sem):
    return pltpu.make_async_copy(src, dst, sem)


def _dispatch_body(ends_ref, nv_ref, pos1_ref, pos2_ref, h_ref, g2_ref, xs_ref, tbuf, zbuf, sems, zsem):
    tb, tm = TB_ROWS, TM_EXP
    i = pl.program_id(0)
    n = pl.num_programs(0)
    slot = i % 2

    def wait_slot(sl):
        for _ in range(2):
            _row_copy(tbuf.at[sl], xs_ref.at[pl.ds(0, tb)], sems.at[sl]).wait()

    @pl.when(i == 0)
    def _():
        zbuf[...] = jnp.zeros(zbuf.shape, f32)

        def zero_tile(t):
            return _row_copy(zbuf, xs_ref.at[pl.ds(pl.multiple_of(t * tm, tm), tm)], zsem)

        def each_fill(fn):
            for e in range(N_EXPERTS):
                first = 0 if e == 0 else ends_ref[e - 1]

                @pl.when(ends_ref[e] > first)
                def _():
                    fn(lax.shift_right_logical(ends_ref[e], tm.bit_length() - 1) - 1)

            def tail(t, carry):
                fn(t)
                return carry

            lax.fori_loop(nv_ref[0], xs_ref.shape[0] // tm, tail, 0)

        each_fill(lambda t: zero_tile(t).start())
        each_fill(lambda t: zero_tile(t).wait())

    @pl.when(i >= 2)
    def _():
        wait_slot(slot)

    h = h_ref[...]
    ms = jnp.mean(h * h, axis=-1, keepdims=True)
    tbuf[slot] = h * lax.rsqrt(ms + EPS) * g2_ref[...]

    for r in range(tb):
        src = tbuf.at[slot, pl.ds(r, 1)]
        _row_copy(src, xs_ref.at[pl.ds(pos1_ref[r], 1)], sems.at[slot]).start(priority=0)
        _row_copy(src, xs_ref.at[pl.ds(pos2_ref[r], 1)], sems.at[slot]).start(priority=1)

    @pl.when(i == n - 1)
    def _():
        wait_slot(slot)

        @pl.when(n >= 2)
        def _():
            wait_slot(1 - slot)


def _dispatch(ends, n_valid, pos1, pos2, h, g2, n_rows):
    s = h.shape[0]
    tb, tm = TB_ROWS, TM_EXP
    assert tm & (tm - 1) == 0 and n_rows % tm == 0
    grid_spec = pltpu.PrefetchScalarGridSpec(
        num_scalar_prefetch=2,
        grid=(s // tb,),
        in_specs=[
            pl.BlockSpec((tb,), lambda i, en, nv: (i,), memory_space=pltpu.SMEM),
            pl.BlockSpec((tb,), lambda i, en, nv: (i,), memory_space=pltpu.SMEM),
            pl.BlockSpec((tb, D_MODEL), lambda i, en, nv: (i, 0)),
            pl.BlockSpec((1, D_MODEL), lambda i, en, nv: (0, 0)),
        ],
        out_specs=pl.BlockSpec(memory_space=pl.ANY),
        scratch_shapes=[
            pltpu.VMEM((2, tb, D_MODEL), f32),
            pltpu.VMEM((tm, D_MODEL), f32),
            pltpu.SemaphoreType.DMA((2,)),
            pltpu.SemaphoreType.DMA,
        ],
    )
    return pl.pallas_call(
        _dispatch_body,
        grid_spec=grid_spec,
        out_shape=jax.ShapeDtypeStruct((n_rows, D_MODEL), f32),
        compiler_params=_cparams(1),
        name="dispatch",
    )(ends, n_valid, pos1, pos2, h, g2)


def _experts_body(te_ref, nv_ref, slot_ref, nxt_ref, xs_ref, wg_hbm, wu_hbm, wd_hbm, y_ref,
                  wg_f, wu_f, wd_f, wg_sc, wu_sc, wd_sc, wsem):
    i = pl.program_id(0)

    def weight_copies(e, sl):
        return [_row_copy(src.at[e], dst.at[sl], wsem.at[sl])
                for src, dst in ((wg_hbm, wg_f), (wu_hbm, wu_f), (wd_hbm, wd_f))]

    @pl.when(i < nv_ref[0])
    def _():
        e = te_ref[i]
        sl = slot_ref[i]
        prev = te_ref[jnp.maximum(i - 1, 0)]

        @pl.when(i == 0)
        def _():
            for c in weight_copies(e, sl):
                c.start()

        @pl.when((i == 0) | (e != prev))
        def _():
            for c in weight_copies(e, sl):
                c.wait()

            @pl.when(nxt_ref[i] >= 0)
            def _():
                for c in weight_copies(nxt_ref[i], 1 - sl):
                    c.start()

            wg_sc[...] = wg_f[sl].astype(bf16)
            wu_sc[...] = wu_f[sl].astype(bf16)
            wd_sc[...] = wd_f[sl].astype(bf16)

        x = xs_ref[...].astype(bf16)
        g = jnp.dot(x, wg_sc[...], preferred_element_type=f32)
        u = jnp.dot(x, wu_sc[...], preferred_element_type=f32)
        hm = (g * jax.nn.sigmoid(g) * u).astype(bf16)
        y_ref[...] = jnp.dot(hm, wd_sc[...], preferred_element_type=f32)

    @pl.when(i >= nv_ref[0])
    def _():
        y_ref[...] = jnp.zeros(y_ref.shape, f32)


def _experts(tile_expert, n_valid, tile_slot, tile_next, xs, w_gate, w_up, w_down):
    n_rows = xs.shape[0]
    tm = TM_EXP
    nt = n_rows // tm
    rows = lambda i, te, nv, ts, tn: (jnp.minimum(i, nv[0] - 1), 0)
    hbm = pl.BlockSpec(memory_space=pl.ANY)
    grid_spec = pltpu.PrefetchScalarGridSpec(
        num_scalar_prefetch=4,
        grid=(nt,),
        in_specs=[pl.BlockSpec((tm, D_MODEL), rows), hbm, hbm, hbm],
        out_specs=pl.BlockSpec((tm, D_MODEL), lambda i, te, nv, ts, tn: (i, 0)),
        scratch_shapes=[
            pltpu.VMEM((2, D_MODEL, D_EXPERT), f32),
            pltpu.VMEM((2, D_MODEL, D_EXPERT), f32),
            pltpu.VMEM((2, D_EXPERT, D_MODEL), f32),
            pltpu.VMEM((D_MODEL, D_EXPERT), bf16),
            pltpu.VMEM((D_MODEL, D_EXPERT), bf16),
            pltpu.VMEM((D_EXPERT, D_MODEL), bf16),
            pltpu.SemaphoreType.DMA((2,)),
        ],
    )
    return pl.pallas_call(
        _experts_body,
        grid_spec=grid_spec,
        out_shape=jax.ShapeDtypeStruct((n_rows, D_MODEL), f32),
        compiler_params=_cparams(1),
        name="experts",
    )(tile_expert, n_valid, tile_slot, tile_next, xs, w_gate, w_up, w_down)


def _combine_body(p1_ref, p2_ref, p1n_ref, p2n_ref, h_ref, r_ref, y_ref, o_ref, ybuf, sems):
    tb = TB_ROWS
    i = pl.program_id(0)
    n = pl.num_programs(0)
    slot = i % 2

    def issue_block(pa_ref, pb_ref, sl):
        for r in range(tb):
            _row_copy(y_ref.at[pl.ds(pa_ref[r], 1)], ybuf.at[sl, 0, pl.ds(r, 1)], sems.at[sl]).start(priority=0)
            _row_copy(y_ref.at[pl.ds(pb_ref[r], 1)], ybuf.at[sl, 1, pl.ds(r, 1)], sems.at[sl]).start(priority=1)

    @pl.when(i == 0)
    def _():
        issue_block(p1_ref, p2_ref, 0)

    @pl.when(i + 1 < n)
    def _():
        issue_block(p1n_ref, p2n_ref, 1 - slot)

    for k in range(2):
        _row_copy(y_ref.at[pl.ds(0, tb)], ybuf.at[slot, k], sems.at[slot]).wait()

    r = r_ref[...]
    o_ref[...] = h_ref[...] + r[:, 2:3] * ybuf[slot, 0] + r[:, 3:4] * ybuf[slot, 1]


def _combine(pos1, pos2, h, r, y):
    s = h.shape[0]
    tb = TB_ROWS
    nb = s // tb
    cur = lambda i: (i,)
    nxt = lambda i: (jnp.minimum(i + 1, nb - 1),)
    smem = functools.partial(pl.BlockSpec, (tb,), memory_space=pltpu.SMEM)
    return pl.pallas_call(
        _combine_body,
        grid=(nb,),
        in_specs=[
            smem(cur), smem(cur), smem(nxt), smem(nxt),
            pl.BlockSpec((tb, D_MODEL), lambda i: (i, 0)),
            pl.BlockSpec((tb, LANES), lambda i: (i, 0)),
            pl.BlockSpec(memory_space=pl.ANY),
        ],
        out_specs=pl.BlockSpec((tb, D_MODEL), lambda i: (i, 0)),
        out_shape=jax.ShapeDtypeStruct((s, D_MODEL), f32),
        scratch_shapes=[pltpu.VMEM((2, 2, tb, D_MODEL), f32), pltpu.SemaphoreType.DMA((2,))],
        compiler_params=_cparams(1),
        name="combine",
    )(pos1, pos2, pos1, pos2, h, r, y)


def kernel(x, norm1_g, w_in, conv_dw_kernel, conv_dw_bias, conv_ln_g, conv_ln_b, q_norm_g, k_norm_g,
           lambda_q1, lambda_k1, lambda_q2, lambda_k2, subln_g, w_out, norm2_g, w_group, b_group,
           w_router, b_router, w_gate, w_up, w_down):
    b, s, d = x.shape
    assert b == 1 and d == D_MODEL and norm1_g.shape[0] == 1
    x2 = x.reshape(s, d)

    w_in_bf = w_in[0].astype(bf16)
    w_out_bf = w_out[0].astype(bf16)
    qg = jnp.tile(q_norm_g[0].reshape(1, 2 * HEAD_DIM), (1, N_HEADS)) * (HEAD_DIM ** -0.5 * math.log2(math.e))
    kg = jnp.tile(k_norm_g[0].reshape(1, 2 * HEAD_DIM), (1, N_HEADS))
    blk = jnp.arange(256) // HEAD_DIM
    bd = jnp.where(blk[:, None] == blk[None, :], 1.0 / HEAD_DIM, 0.0).astype(bf16)
    n_r = N_GROUPS + N_EXPERTS
    wr = jnp.concatenate(
        [w_group[0], jnp.transpose(w_router[0], (1, 0, 2)).reshape(d, N_EXPERTS),
         jnp.zeros((d, LANES - n_r), f32)], axis=1)
    wr_hi = wr.astype(bf16)
    wr = jnp.concatenate([wr_hi, (wr - wr_hi.astype(f32)).astype(bf16)], axis=1)
    br = jnp.concatenate([b_group[0], b_router[0].reshape(N_EXPERTS), jnp.zeros((LANES - n_r,), f32)])[None]
    row = lambda a: a.reshape(1, -1)

    y_conv, qt, k, vt = _inproj(x2, norm1_g, w_in_bf, qg, kg, bd, conv_dw_kernel[0], row(conv_dw_bias[0]),
                                row(conv_ln_g[0]), row(conv_ln_b[0]))
    y_attn = _attn(qt, k, vt, row(lambda_q1[0]), row(lambda_k1[0]), row(lambda_q2[0]), row(lambda_k2[0]),
                   row(subln_g[0]))
    h, r = _outproj(y_conv, y_attn, x2, w_out_bf, norm2_g, wr, br)

    pos, cnt = _rank(r)
    pos1 = pos[0]
    pos2 = pos[1]
    tm = TM_EXP
    counts = cnt[0, :N_EXPERTS].astype(jnp.int32)
    padded = ((counts + tm - 1) // tm) * tm
    ends = jnp.cumsum(padded)
    n_rows = TOP_K * s + N_EXPERTS * tm
    nt = n_rows // tm
    n_valid = (ends[-1] // tm).astype(jnp.int32)
    tile_start = jnp.minimum(jnp.arange(nt, dtype=jnp.int32), n_valid - 1) * tm
    tile_expert = jnp.sum(tile_start[:, None] >= ends[None, :], axis=1).astype(jnp.int32)

    n_valid = n_valid.reshape(1)
    eidx = jnp.arange(N_EXPERTS, dtype=jnp.int32)
    nonempty = counts > 0
    slot_e = (jnp.cumsum(nonempty.astype(jnp.int32)) - 1) % 2
    later = jnp.where(nonempty[None, :] & (eidx[None, :] > eidx[:, None]), eidx[None, :], N_EXPERTS)
    next_e = jnp.min(later, axis=1)
    next_e = jnp.where(next_e == N_EXPERTS, -1, next_e).astype(jnp.int32)
    is_e = tile_expert[:, None] == eidx[None, :]
    tile_slot = jnp.sum(jnp.where(is_e, slot_e[None, :], 0), axis=1).astype(jnp.int32)
    tile_next = jnp.sum(jnp.where(is_e, next_e[None, :], 0), axis=1).astype(jnp.int32)

    xs = _dispatch(ends.astype(jnp.int32), n_valid, pos1, pos2, h, norm2_g, n_rows)
    ys = _experts(tile_expert, n_valid, tile_slot, tile_next, xs, w_gate[0], w_up[0], w_down[0])
    out = _combine(pos1, pos2, h, r, ys)
    return out.reshape(b, s, d)
```

```python
import functools
import math

import jax
import jax.numpy as jnp
from jax import lax
from jax.experimental import pallas as pl
from jax.experimental.pallas import tpu as pltpu

D_MODEL = 2048
D_CONV = 1024
N_HEADS = 8
HEAD_DIM = 64
V_DIM = 128
CONV_WIDTH = 31
CHUNK = 64
N_GROUPS = 4
EXPERTS_PER_GROUP = 8
N_EXPERTS = 32
TOP_K = 2
D_EXPERT = 512
D_Q = 1024
D_V = 1024
D_IN = 2 * D_CONV + 2 * D_Q + D_V
EPS = 1e-6
LAMBDA_INIT = 0.8 - 0.6 * math.exp(-0.3 * 0)

LANES = 128
SUBLANES = 8
VMEM_LIMIT = 56 * 1024 * 1024
NEG_BIG = -1e30

TM_PROJ = 512
TK_ATTN = 512
TQ_ATTN = 512
RC_CONV = 64
HALO = 32
TB_RANK = 512
TB_ROWS = 256
TM_EXP = 256

bf16 = jnp.bfloat16
f32 = jnp.float32


def _cparams(n_axes):
    return pltpu.CompilerParams(
        dimension_semantics=("arbitrary",) * n_axes, vmem_limit_bytes=VMEM_LIMIT)


def _inproj_body(x_ref, g1_ref, w_ref, qg_ref, kg_ref, bd_ref, cw_ref, cb_ref, lg_ref, lb_ref,
                 yc_ref, qt_ref, k_ref, vt_ref, pad_sc, acc_sc):
    i = pl.program_id(0)
    tm, rc = TM_PROJ, RC_CONV

    @pl.when(i == 0)
    def _():
        pad_sc[0:HALO, :] = jnp.zeros((HALO, D_CONV), f32)

    pad_sc[HALO + tm:, :] = jnp.zeros((SUBLANES, D_CONV), f32)

    x = x_ref[...]
    ms = jnp.mean(x * x, axis=-1, keepdims=True)
    xn = (x * lax.rsqrt(ms + EPS) * g1_ref[...]).astype(bf16)

    def mm(c0, n):
        return jnp.dot(xn, w_ref[:, c0:c0 + n], preferred_element_type=f32)

    nt = 512
    for c in range(0, D_CONV, nt):
        a = mm(c, nt)
        b = mm(D_CONV + c, nt)
        pad_sc[HALO:HALO + tm, c:c + nt] = a * jax.nn.sigmoid(b)

    bd = bd_ref[...]
    sl = 256

    def qk_norm(u, gain):
        ms64 = jnp.dot((u * u).astype(bf16), bd, preferred_element_type=f32)
        return u * lax.rsqrt(ms64 + EPS) * gain

    uq = mm(2 * D_CONV, D_Q)
    uk = mm(2 * D_CONV + D_Q, D_Q)
    uv = mm(2 * D_CONV + 2 * D_Q, D_V)
    for c in range(0, D_Q, sl):
        qn = qk_norm(uq[:, c:c + sl], qg_ref[:, c:c + sl])
        qt_ref[c:c + sl, :] = qn.T.astype(bf16)
    for c in range(0, D_Q, sl):
        k_ref[:, c:c + sl] = qk_norm(uk[:, c:c + sl], kg_ref[:, c:c + sl]).astype(bf16)
    for c in range(0, D_V, sl):
        vt_ref[0, c:c + sl, :] = uv[:, c:c + sl].T.astype(bf16)

    off = HALO - (CONV_WIDTH - 1)
    for c0 in range(0, D_CONV, LANES):
        cols = slice(c0, c0 + LANES)
        for r0 in range(0, tm, rc):
            acc = jnp.broadcast_to(cb_ref[:, cols], (rc, LANES))
            for a in range(SUBLANES):
                qa = None
                for jp in range(a, off + CONV_WIDTH, SUBLANES):
                    j = jp - off
                    if j < 0:
                        continue
                    base = r0 + jp - a
                    term = cw_ref[j:j + 1, cols] * pad_sc[base:base + rc + SUBLANES, cols]
                    qa = term if qa is None else qa + term
                acc = acc + qa[a:a + rc, :]
            acc_sc[r0:r0 + rc, cols] = acc
    pad_sc[0:HALO, :] = pad_sc[tm:tm + HALO, :]
    acc = acc_sc[...]
    mu = jnp.mean(acc, axis=-1, keepdims=True)
    xc = acc - mu
    var = jnp.mean(xc * xc, axis=-1, keepdims=True)
    y = xc * lax.rsqrt(var + EPS) * lg_ref[...] + lb_ref[...]
    yc_ref[...] = (y * jax.nn.sigmoid(y)).astype(bf16)


def _inproj(x2, g1, w_bf, qg, kg, bd, cw, cb, lg, lb):
    s = x2.shape[0]
    tm = TM_PROJ
    assert tm == TK_ATTN and s % tm == 0 and tm % RC_CONV == 0 and HALO >= CONV_WIDTH - 1
    const = lambda i: (0, 0)
    return pl.pallas_call(
        _inproj_body,
        grid=(s // tm,),
        in_specs=[
            pl.BlockSpec((tm, D_MODEL), lambda i: (i, 0)),
            pl.BlockSpec((1, D_MODEL), const),
            pl.BlockSpec((D_MODEL, D_IN), const, pipeline_mode=pl.Buffered(1)),
            pl.BlockSpec((1, D_Q), const),
            pl.BlockSpec((1, D_Q), const),
            pl.BlockSpec((256, 256), const),
            pl.BlockSpec((CONV_WIDTH, D_CONV), const),
            pl.BlockSpec((1, D_CONV), const),
            pl.BlockSpec((1, D_CONV), const),
            pl.BlockSpec((1, D_CONV), const),
        ],
        out_specs=[
            pl.BlockSpec((tm, D_CONV), lambda i: (i, 0)),
            pl.BlockSpec((D_Q, tm), lambda i: (0, i)),
            pl.BlockSpec((tm, D_Q), lambda i: (i, 0)),
            pl.BlockSpec((1, D_V, tm), lambda i: (i, 0, 0)),
        ],
        out_shape=[
            jax.ShapeDtypeStruct((s, D_CONV), bf16),
            jax.ShapeDtypeStruct((D_Q, s), bf16),
            jax.ShapeDtypeStruct((s, D_Q), bf16),
            jax.ShapeDtypeStruct((s // tm, D_V, tm), bf16),
        ],
        scratch_shapes=[pltpu.VMEM((HALO + tm + SUBLANES, D_CONV), f32), pltpu.VMEM((tm, D_CONV), f32)],
        compiler_params=_cparams(1),
        name="inproj",
    )(x2, g1, w_bf, qg, kg, bd, cw, cb, lg, lb)


def _attn_body(qt_ref, k_ref, vt_ref, lq1_ref, lk1_ref, lq2_ref, lk2_ref, sg_ref,
               o_ref, m_sc, l_sc, acc_sc, sa_sc, sb_sc):
    tq, tk = TQ_ATTN, TK_ATTN
    i = pl.program_id(1)
    qt = qt_ref[...]
    sub = lax.broadcasted_iota(jnp.int32, qt.shape, 0)
    zero = jnp.zeros_like(qt)
    qst = jnp.concatenate([jnp.where(sub < HEAD_DIM, qt, zero),
                           jnp.where(sub >= HEAD_DIM, qt, zero)], axis=1)

    m_sc[...] = jnp.full(m_sc.shape, NEG_BIG, f32)
    l_sc[...] = jnp.zeros(l_sc.shape, f32)
    acc_sc[...] = jnp.zeros(acc_sc.shape, f32)

    def scores(kb, dst):
        kblk = k_ref[pl.ds(pl.multiple_of(kb * tk, tk), tk), :]
        dst[...] = jnp.dot(kblk, qst, preferred_element_type=f32)

    def consume(src, kb, masked):
        vt = vt_ref[kb]
        for c0 in (0, tq):
            cols = slice(c0, c0 + tq)
            s = src[:, cols]
            if masked:
                key = lax.broadcasted_iota(jnp.int32, s.shape, 0)
                qry = lax.broadcasted_iota(jnp.int32, s.shape, 1)
                s = jnp.where((key // CHUNK) <= (qry // CHUNK), s, NEG_BIG)
            m_old = m_sc[:, cols]
            m_new = jnp.maximum(m_old, jnp.max(s, axis=0, keepdims=True))
            alpha = jnp.exp2(m_old - m_new)
            p = jnp.exp2(s - m_new)
            l_sc[:, cols] = alpha * l_sc[:, cols] + jnp.sum(p, axis=0, keepdims=True)
            acc_sc[:, cols] = alpha * acc_sc[:, cols] + jnp.dot(vt, p.astype(bf16),
                                                                preferred_element_type=f32)
            m_sc[:, cols] = m_new

    last_full = jnp.maximum(i - 1, 0)
    scores(i, sa_sc)
    scores(0, sb_sc)
    consume(sa_sc, i, True)

    def pair(j, carry):
        scores(2 * j + 1, sa_sc)
        consume(sb_sc, 2 * j, False)
        scores(jnp.minimum(2 * j + 2, last_full), sb_sc)
        consume(sa_sc, 2 * j + 1, False)
        return carry

    lax.fori_loop(0, i // 2, pair, 0)

    @pl.when(i % 2 == 1)
    def _():
        consume(sb_sc, i - 1, False)

    lam = (jnp.exp(jnp.sum(lq1_ref[...] * lk1_ref[...], axis=-1, keepdims=True))
           - jnp.exp(jnp.sum(lq2_ref[...] * lk2_ref[...], axis=-1, keepdims=True))
           + LAMBDA_INIT)
    o_all = acc_sc[...] / l_sc[...]
    o = (o_all[:, :tq] - lam * o_all[:, tq:]).T
    ms = jnp.mean(o * o, axis=-1, keepdims=True)
    y = o * lax.rsqrt(ms + EPS) * sg_ref[...] * (1.0 - LAMBDA_INIT)
    o_ref[...] = y.astype(bf16)


def _attn(qt, k, vt, lq1, lk1, lq2, lk2, sg):
    s = k.shape[0]
    tq, tk = TQ_ATTN, TK_ATTN
    assert tq == tk and s % tq == 0
    vec = lambda h, i: (0, 0)
    return pl.pallas_call(
        _attn_body,
        grid=(N_HEADS, s // tq),
        in_specs=[
            pl.BlockSpec((2 * HEAD_DIM, tq), lambda h, i: (h, i)),
            pl.BlockSpec((s, 2 * HEAD_DIM), lambda h, i: (0, h)),
            pl.BlockSpec((s // tk, V_DIM, tk), lambda h, i: (0, h, 0)),
            pl.BlockSpec((1, HEAD_DIM), vec),
            pl.BlockSpec((1, HEAD_DIM), vec),
            pl.BlockSpec((1, HEAD_DIM), vec),
            pl.BlockSpec((1, HEAD_DIM), vec),
            pl.BlockSpec((1, V_DIM), vec),
        ],
        out_specs=pl.BlockSpec((tq, V_DIM), lambda h, i: (i, h)),
        out_shape=jax.ShapeDtypeStruct((s, D_V), bf16),
        scratch_shapes=[
            pltpu.VMEM((1, 2 * tq), f32),
            pltpu.VMEM((1, 2 * tq), f32),
            pltpu.VMEM((V_DIM, 2 * tq), f32),
            pltpu.VMEM((tk, 2 * tq), f32),
            pltpu.VMEM((tk, 2 * tq), f32),
        ],
        compiler_params=_cparams(2),
        name="attn",
    )(qt, k, vt, lq1, lk1, lq2, lk2, sg)


def _outproj_body(yc_ref, ya_ref, x_ref, w_ref, g2_ref, wr_ref, br_ref, h_ref, r_ref):
    half = TM_PROJ // 2
    for r0 in range(0, TM_PROJ, half):
        rows = slice(r0, r0 + half)
        h = (jnp.dot(yc_ref[rows, :], w_ref[0:D_CONV, :], preferred_element_type=f32)
             + jnp.dot(ya_ref[rows, :], w_ref[D_CONV:, :], preferred_element_type=f32)
             + x_ref[rows, :])
        h_ref[rows, :] = h
        ms = jnp.mean(h * h, axis=-1, keepdims=True)
        t = h * lax.rsqrt(ms + EPS) * g2_ref[...]
        t_hi = t.astype(bf16)
        t_lo = (t - t_hi.astype(f32)).astype(bf16)
        hh = jnp.dot(t_hi, wr_ref[...], preferred_element_type=f32)
        lh = jnp.dot(t_lo, wr_ref[:, 0:LANES], preferred_element_type=f32)
        logits = hh[:, 0:LANES] + hh[:, LANES:] + lh + br_ref[...]
        r_ref[rows, :] = _route(logits)


def _route(logits):
    lane = lax.broadcasted_iota(jnp.int32, logits.shape, 1)
    lanef = lane.astype(f32)
    big = float(LANES)
    is_g = lane < N_GROUPS
    gl = jnp.where(is_g, logits, NEG_BIG)
    gmax = jnp.max(gl, axis=1, keepdims=True)
    gidx = jnp.min(jnp.where(gl == gmax, lanef, big), axis=1, keepdims=True)
    gsum = jnp.sum(jnp.where(is_g, jnp.exp(gl - gmax), 0.0), axis=1, keepdims=True)
    gw = 1.0 / gsum

    e_of_lane = lane - N_GROUPS
    in_group = ((e_of_lane >= 0) & (e_of_lane < N_EXPERTS)
                & ((e_of_lane // EXPERTS_PER_GROUP).astype(f32) == gidx))
    el = jnp.where(in_group, logits, NEG_BIG)
    v1 = jnp.max(el, axis=1, keepdims=True)
    i1 = jnp.min(jnp.where(el == v1, lanef, big), axis=1, keepdims=True)
    el2 = jnp.where(lanef == i1, NEG_BIG, el)
    v2 = jnp.max(el2, axis=1, keepdims=True)
    i2 = jnp.min(jnp.where(el2 == v2, lanef, big), axis=1, keepdims=True)
    e21 = jnp.exp(v2 - v1)
    den = 1.0 + e21
    w1 = gw * (1.0 / den)
    w2 = gw * (e21 / den)
    return jnp.where(lane == 0, i1 - N_GROUPS,
                     jnp.where(lane == 1, i2 - N_GROUPS,
                               jnp.where(lane == 2, w1, jnp.where(lane == 3, w2, 0.0))))


def _outproj(yc, ya, x2, w_bf, g2, wr, br):
    s = x2.shape[0]
    tm = TM_PROJ
    const = lambda i: (0, 0)
    row = lambda i: (i, 0)
    return pl.pallas_call(
        _outproj_body,
        grid=(s // tm,),
        in_specs=[
            pl.BlockSpec((tm, D_CONV), row),
            pl.BlockSpec((tm, D_V), row),
            pl.BlockSpec((tm, D_MODEL), row),
            pl.BlockSpec((D_CONV + D_V, D_MODEL), const, pipeline_mode=pl.Buffered(1)),
            pl.BlockSpec((1, D_MODEL), const),
            pl.BlockSpec((D_MODEL, 2 * LANES), const),
            pl.BlockSpec((1, LANES), const),
        ],
        out_specs=[pl.BlockSpec((tm, D_MODEL), row), pl.BlockSpec((tm, LANES), row)],
        out_shape=[jax.ShapeDtypeStruct((s, D_MODEL), f32), jax.ShapeDtypeStruct((s, LANES), f32)],
        compiler_params=_cparams(1),
        name="outproj",
    )(yc, ya, x2, w_bf, g2, wr, br)


def _rank_body(r_ref, pos_ref, cnt_ref, run_sc, rank_sc):
    tb = TB_RANK
    ph = pl.program_id(0)
    i = pl.program_id(1)
    r = r_ref[...]
    lane = lax.broadcasted_iota(jnp.int32, r.shape, 1)
    lanef = lane.astype(f32)
    oh1 = (lanef == r[:, 0:1]).astype(f32)
    oh2 = (lanef == r[:, 1:2]).astype(f32)

    @pl.when((ph == 0) & (i == 0))
    def _():
        run_sc[...] = jnp.zeros(run_sc.shape, f32)

    @pl.when(ph == 0)
    def _():
        both = oh1 + oh2
        row = lax.broadcasted_iota(jnp.int32, (tb, tb), 0)
        col = lax.broadcasted_iota(jnp.int32, (tb, tb), 1)
        tri = (row > col).astype(bf16)
        before = jnp.dot(tri, both.astype(bf16), preferred_element_type=f32) + run_sc[...]
        rank1 = jnp.sum(before * oh1, axis=1, keepdims=True)
        rank2 = jnp.sum(before * oh2, axis=1, keepdims=True)
        rank_sc[i] = jnp.where(lane == 0, rank1, jnp.where(lane == 1, rank2, 0.0))
        run_sc[...] = run_sc[...] + jnp.sum(both, axis=0, keepdims=True)

    @pl.when(ph == 1)
    def _():
        cnt = run_sc[...]
        tiles = jnp.floor((cnt + (TM_EXP - 1)) * (1.0 / TM_EXP))
        lr = lax.broadcasted_iota(jnp.int32, (LANES, LANES), 0)
        lc = lax.broadcasted_iota(jnp.int32, (LANES, LANES), 1)
        upper = (lr < lc).astype(bf16)
        tiles8 = jnp.broadcast_to(tiles, (SUBLANES, LANES)).astype(bf16)
        offs = jnp.dot(tiles8, upper, preferred_element_type=f32)[0:1, :] * TM_EXP
        rk = rank_sc[i]
        pos1 = jnp.sum(oh1 * offs, axis=1, keepdims=True) + rk[:, 0:1]
        pos2 = jnp.sum(oh2 * offs, axis=1, keepdims=True) + rk[:, 1:2]
        pp = jnp.where(lane == 0, pos1, jnp.where(lane == 1, pos2, 0.0))
        pos_ref[...] = pp.T[0:SUBLANES, :].astype(jnp.int32)
        cnt_ref[...] = cnt


def _rank(r):
    s = r.shape[0]
    tb = TB_RANK
    nb = s // tb
    return pl.pallas_call(
        _rank_body,
        grid=(2, nb),
        in_specs=[pl.BlockSpec((tb, LANES), lambda ph, i: (i, 0))],
        out_specs=[pl.BlockSpec((SUBLANES, tb), lambda ph, i: (0, i * ph)),
                   pl.BlockSpec((1, LANES), lambda ph, i: (0, 0))],
        out_shape=[jax.ShapeDtypeStruct((SUBLANES, s), jnp.int32), jax.ShapeDtypeStruct((1, LANES), f32)],
        scratch_shapes=[pltpu.VMEM((1, LANES), f32), pltpu.VMEM((nb, tb, LANES), f32)],
        compiler_params=_cparams(2),
        name="rank",
    )(r)


def _row_copy(src, dst, sem):
    return pltpu.make_async_copy(src, dst, sem)


def _dispatch_body(ends_ref, nv_ref, pos1_ref, pos2_ref, h_ref, g2_ref, xs_ref, tbuf, zbuf, sems, zsem):
    tb, tm = TB_ROWS, TM_EXP
    i = pl.program_id(0)
    n = pl.num_programs(0)
    slot = i % 2

    def wait_slot(sl):
        for _ in range(2):
            _row_copy(tbuf.at[sl], xs_ref.at[pl.ds(0, tb)], sems.at[sl]).wait()

    @pl.when(i == 0)
    def _():
        zbuf[...] = jnp.zeros(zbuf.shape, f32)

        def zero_tile(t):
            return _row_copy(zbuf, xs_ref.at[pl.ds(pl.multiple_of(t * tm, tm), tm)], zsem)

        def each_fill(fn):
            for e in range(N_EXPERTS):
                first = 0 if e == 0 else ends_ref[e - 1]

                @pl.when(ends_ref[e] > first)
                def _():
                    fn(lax.shift_right_logical(ends_ref[e], tm.bit_length() - 1) - 1)

            def tail(t, carry):
                fn(t)
                return carry

            lax.fori_loop(nv_ref[0], xs_ref.shape[0] // tm, tail, 0)

        each_fill(lambda t: zero_tile(t).start())
        each_fill(lambda t: zero_tile(t).wait())

    @pl.when(i >= 2)
    def _():
        wait_slot(slot)

    h = h_ref[...]
    ms = jnp.mean(h * h, axis=-1, keepdims=True)
    tbuf[slot] = h * lax.rsqrt(ms + EPS) * g2_ref[...]

    for r in range(tb):
        src = tbuf.at[slot, pl.ds(r, 1)]
        _row_copy(src, xs_ref.at[pl.ds(pos1_ref[r], 1)], sems.at[slot]).start(priority=0)
        _row_copy(src, xs_ref.at[pl.ds(pos2_ref[r], 1)], sems.at[slot]).start(priority=1)

    @pl.when(i == n - 1)
    def _():
        wait_slot(slot)

        @pl.when(n >= 2)
        def _():
            wait_slot(1 - slot)


def _dispatch(ends, n_valid, pos1, pos2, h, g2, n_rows):
    s = h.shape[0]
    tb, tm = TB_ROWS, TM_EXP
    assert tm & (tm - 1) == 0 and n_rows % tm == 0
    grid_spec = pltpu.PrefetchScalarGridSpec(
        num_scalar_prefetch=2,
        grid=(s // tb,),
        in_specs=[
            pl.BlockSpec((tb,), lambda i, en, nv: (i,), memory_space=pltpu.SMEM),
            pl.BlockSpec((tb,), lambda i, en, nv: (i,), memory_space=pltpu.SMEM),
            pl.BlockSpec((tb, D_MODEL), lambda i, en, nv: (i, 0)),
            pl.BlockSpec((1, D_MODEL), lambda i, en, nv: (0, 0)),
        ],
        out_specs=pl.BlockSpec(memory_space=pl.ANY),
        scratch_shapes=[
            pltpu.VMEM((2, tb, D_MODEL), f32),
            pltpu.VMEM((tm, D_MODEL), f32),
            pltpu.SemaphoreType.DMA((2,)),
            pltpu.SemaphoreType.DMA,
        ],
    )
    return pl.pallas_call(
        _dispatch_body,
        grid_spec=grid_spec,
        out_shape=jax.ShapeDtypeStruct((n_rows, D_MODEL), f32),
        compiler_params=_cparams(1),
        name="dispatch",
    )(ends, n_valid, pos1, pos2, h, g2)


def _experts_body(te_ref, nv_ref, slot_ref, nxt_ref, xs_ref, wg_hbm, wu_hbm, wd_hbm, y_ref,
                  wg_f, wu_f, wd_f, wg_sc, wu_sc, wd_sc, wsem):
    i = pl.program_id(0)

    def weight_copies(e, sl):
        return [_row_copy(src.at[e], dst.at[sl], wsem.at[sl])
                for src, dst in ((wg_hbm, wg_f), (wu_hbm, wu_f), (wd_hbm, wd_f))]

    @pl.when(i < nv_ref[0])
    def _():
        e = te_ref[i]
        sl = slot_ref[i]
        prev = te_ref[jnp.maximum(i - 1, 0)]

        @pl.when(i == 0)
        def _():
            for c in weight_copies(e, sl):
                c.start()

        @pl.when((i == 0) | (e != prev))
        def _():
            for c in weight_copies(e, sl):
                c.wait()

            @pl.when(nxt_ref[i] >= 0)
            def _():
                for c in weight_copies(nxt_ref[i], 1 - sl):
                    c.start()

            wg_sc[...] = wg_f[sl].astype(bf16)
            wu_sc[...] = wu_f[sl].astype(bf16)
            wd_sc[...] = wd_f[sl].astype(bf16)

        x = xs_ref[...].astype(bf16)
        g = jnp.dot(x, wg_sc[...], preferred_element_type=f32)
        u = jnp.dot(x, wu_sc[...], preferred_element_type=f32)
        hm = (g * jax.nn.sigmoid(g) * u).astype(bf16)
        y_ref[...] = jnp.dot(hm, wd_sc[...], preferred_element_type=f32)

    @pl.when(i >= nv_ref[0])
    def _():
        y_ref[...] = jnp.zeros(y_ref.shape, f32)


def _experts(tile_expert, n_valid, tile_slot, tile_next, xs, w_gate, w_up, w_down):
    n_rows = xs.shape[0]
    tm = TM_EXP
    nt = n_rows // tm
    rows = lambda i, te, nv, ts, tn: (jnp.minimum(i, nv[0] - 1), 0)
    hbm = pl.BlockSpec(memory_space=pl.ANY)
    grid_spec = pltpu.PrefetchScalarGridSpec(
        num_scalar_prefetch=4,
        grid=(nt,),
        in_specs=[pl.BlockSpec((tm, D_MODEL), rows), hbm, hbm, hbm],
        out_specs=pl.BlockSpec((tm, D_MODEL), lambda i, te, nv, ts, tn: (i, 0)),
        scratch_shapes=[
            pltpu.VMEM((2, D_MODEL, D_EXPERT), f32),
            pltpu.VMEM((2, D_MODEL, D_EXPERT), f32),
            pltpu.VMEM((2, D_EXPERT, D_MODEL), f32),
            pltpu.VMEM((D_MODEL, D_EXPERT), bf16),
            pltpu.VMEM((D_MODEL, D_EXPERT), bf16),
            pltpu.VMEM((D_EXPERT, D_MODEL), bf16),
            pltpu.SemaphoreType.DMA((2,)),
        ],
    )
    return pl.pallas_call(
        _experts_body,
        grid_spec=grid_spec,
        out_shape=jax.ShapeDtypeStruct((n_rows, D_MODEL), f32),
        compiler_params=_cparams(1),
        name="experts",
    )(tile_expert, n_valid, tile_slot, tile_next, xs, w_gate, w_up, w_down)


def _combine_body(p1_ref, p2_ref, p1n_ref, p2n_ref, h_ref, r_ref, y_ref, o_ref, ybuf, sems):
    tb = TB_ROWS
    i = pl.program_id(0)
    n = pl.num_programs(0)
    slot = i % 2

    def issue_block(pa_ref, pb_ref, sl):
        for r in range(tb):
            _row_copy(y_ref.at[pl.ds(pa_ref[r], 1)], ybuf.at[sl, 0, pl.ds(r, 1)], sems.at[sl]).start(priority=0)
            _row_copy(y_ref.at[pl.ds(pb_ref[r], 1)], ybuf.at[sl, 1, pl.ds(r, 1)], sems.at[sl]).start(priority=1)

    @pl.when(i == 0)
    def _():
        issue_block(p1_ref, p2_ref, 0)

    @pl.when(i + 1 < n)
    def _():
        issue_block(p1n_ref, p2n_ref, 1 - slot)

    for k in range(2):
        _row_copy(y_ref.at[pl.ds(0, tb)], ybuf.at[slot, k], sems.at[slot]).wait()

    r = r_ref[...]
    o_ref[...] = h_ref[...] + r[:, 2:3] * ybuf[slot, 0] + r[:, 3:4] * ybuf[slot, 1]


def _combine(pos1, pos2, h, r, y):
    s = h.shape[0]
    tb = TB_ROWS
    nb = s // tb
    cur = lambda i: (i,)
    nxt = lambda i: (jnp.minimum(i + 1, nb - 1),)
    smem = functools.partial(pl.BlockSpec, (tb,), memory_space=pltpu.SMEM)
    return pl.pallas_call(
        _combine_body,
        grid=(nb,),
        in_specs=[
            smem(cur), smem(cur), smem(nxt), smem(nxt),
            pl.BlockSpec((tb, D_MODEL), lambda i: (i, 0)),
            pl.BlockSpec((tb, LANES), lambda i: (i, 0)),
            pl.BlockSpec(memory_space=pl.ANY),
        ],
        out_specs=pl.BlockSpec((tb, D_MODEL), lambda i: (i, 0)),
        out_shape=jax.ShapeDtypeStruct((s, D_MODEL), f32),
        scratch_shapes=[pltpu.VMEM((2, 2, tb, D_MODEL), f32), pltpu.SemaphoreType.DMA((2,))],
        compiler_params=_cparams(1),
        name="combine",
    )(pos1, pos2, pos1, pos2, h, r, y)


def kernel(x, norm1_g, w_in, conv_dw_kernel, conv_dw_bias, conv_ln_g, conv_ln_b, q_norm_g, k_norm_g,
           lambda_q1, lambda_k1, lambda_q2, lambda_k2, subln_g, w_out, norm2_g, w_group, b_group,
           w_router, b_router, w_gate, w_up, w_down):
    b, s, d = x.shape
    assert b == 1 and d == D_MODEL and norm1_g.shape[0] == 1
    x2 = x.reshape(s, d)

    w_in_bf = w_in[0].astype(bf16)
    w_out_bf = w_out[0].astype(bf16)
    qg = jnp.tile(q_norm_g[0].reshape(1, 2 * HEAD_DIM), (1, N_HEADS)) * (HEAD_DIM ** -0.5 * math.log2(math.e))
    kg = jnp.tile(k_norm_g[0].reshape(1, 2 * HEAD_DIM), (1, N_HEADS))
    blk = jnp.arange(256) // HEAD_DIM
    bd = jnp.where(blk[:, None] == blk[None, :], 1.0 / HEAD_DIM, 0.0).astype(bf16)
    n_r = N_GROUPS + N_EXPERTS
    wr = jnp.concatenate(
        [w_group[0], jnp.transpose(w_router[0], (1, 0, 2)).reshape(d, N_EXPERTS),
         jnp.zeros((d, LANES - n_r), f32)], axis=1)
    wr_hi = wr.astype(bf16)
    wr = jnp.concatenate([wr_hi, (wr - wr_hi.astype(f32)).astype(bf16)], axis=1)
    br = jnp.concatenate([b_group[0], b_router[0].reshape(N_EXPERTS), jnp.zeros((LANES - n_r,), f32)])[None]
    row = lambda a: a.reshape(1, -1)

    y_conv, qt, k, vt = _inproj(x2, norm1_g, w_in_bf, qg, kg, bd, conv_dw_kernel[0], row(conv_dw_bias[0]),
                                row(conv_ln_g[0]), row(conv_ln_b[0]))
    y_attn = _attn(qt, k, vt, row(lambda_q1[0]), row(lambda_k1[0]), row(lambda_q2[0]), row(lambda_k2[0]),
                   row(subln_g[0]))
    h, r = _outproj(y_conv, y_attn, x2, w_out_bf, norm2_g, wr, br)

    pos, cnt = _rank(r)
    pos1 = pos[0]
    pos2 = pos[1]
    tm = TM_EXP
    counts = cnt[0, :N_EXPERTS].astype(jnp.int32)
    padded = ((counts + tm - 1) // tm) * tm
    ends = jnp.cumsum(padded)
    n_rows = TOP_K * s + N_EXPERTS * tm
    nt = n_rows // tm
    n_valid = (ends[-1] // tm).astype(jnp.int32)
    tile_start = jnp.minimum(jnp.arange(nt, dtype=jnp.int32), n_valid - 1) * tm
    tile_expert = jnp.sum(tile_start[:, None] >= ends[None, :], axis=1).astype(jnp.int32)

    n_valid = n_valid.reshape(1)
    eidx = jnp.arange(N_EXPERTS, dtype=jnp.int32)
    nonempty = counts > 0
    slot_e = (jnp.cumsum(nonempty.astype(jnp.int32)) - 1) % 2
    later = jnp.where(nonempty[None, :] & (eidx[None, :] > eidx[:, None]), eidx[None, :], N_EXPERTS)
    next_e = jnp.min(later, axis=1)
    next_e = jnp.where(next_e == N_EXPERTS, -1, next_e).astype(jnp.int32)
    is_e = tile_expert[:, None] == eidx[None, :]
    tile_slot = jnp.sum(jnp.where(is_e, slot_e[None, :], 0), axis=1).astype(jnp.int32)
    tile_next = jnp.sum(jnp.where(is_e, next_e[None, :], 0), axis=1).astype(jnp.int32)

    xs = _dispatch(ends.astype(jnp.int32), n_valid, pos1, pos2, h, norm2_g, n_rows)
    ys = _experts(tile_expert, n_valid, tile_slot, tile_next, xs, w_gate[0], w_up[0], w_down[0])
    out = _combine(pos1, pos2, h, r, ys)
    return out.reshape(b, s, d)
```
